```python
import jax, jax.numpy as jnp
from jax import lax
import numpy as np

D_MODEL = 1024
BATCH = 16
SEQ = 2048
DEPTH = 4

POOL_WIDTH = D_MODEL
N_POOL_GROUPS = 4
POOL_WINDOWS = (2, 4, 8, 16)
POOL_GROUP = POOL_WIDTH // N_POOL_GROUPS
RWKV_WIDTH = D_MODEL
HEAD_SIZE = 64
N_HEADS = RWKV_WIDTH // HEAD_SIZE
D_DECAY_LORA = 64
D_AAA_LORA = 64
D_MV_LORA = 32
N_BRANCHES = 2
NORM_EPS = 1e-6
LNX_EPS = 1e-5 * HEAD_SIZE
SHIFT_WIDTH = 3 * RWKV_WIDTH + D_DECAY_LORA + D_AAA_LORA
IN_SPLITS = (POOL_WIDTH, POOL_WIDTH, SHIFT_WIDTH, RWKV_WIDTH, N_BRANCHES * D_MODEL)
IN_WIDTH = 2 * POOL_WIDTH + SHIFT_WIDTH + RWKV_WIDTH + N_BRANCHES * D_MODEL

kernel_name = 'pool_rwkv7_gated_hybrid'


def rms_norm(x, g):
    xf = x.astype(jnp.float32)
    y = xf * lax.rsqrt(jnp.mean(xf * xf, axis=-1, keepdims=True) + NORM_EPS)
    return (y * g.astype(jnp.float32)).astype(x.dtype)


def split_cols(z, widths):
    points = [int(p) for p in np.cumsum(widths)[:-1]]
    return jnp.split(z, points, axis=-1)


def token_shift(z, mu):
    z_prev = jnp.pad(z, ((0, 0), (1, 0), (0, 0)))[:, :-1]
    return z + mu * (z_prev - z)


def pool_mixer(u, lin, scale):
    b, s, _ = u.shape
    groups = u.astype(jnp.float32).reshape(b, s, N_POOL_GROUPS, POOL_GROUP)
    csum = jnp.cumsum(groups, axis=1)
    pos = jnp.arange(1, s + 1, dtype=jnp.float32)
    pooled = []
    for gi, w in enumerate(POOL_WINDOWS):
        c = csum[:, :, gi]
        c_lag = jnp.pad(c, ((0, 0), (w, 0), (0, 0)))[:, :s]
        count = jnp.minimum(pos, float(w))[None, :, None]
        pooled.append((c - c_lag) / count)
    pooled = jnp.stack(pooled, axis=2)
    mixed = jnp.einsum('bsgc,gcd->bsgd', (pooled - groups).astype(u.dtype), lin)
    return mixed.reshape(b, s, POOL_WIDTH) * scale


def rwkv7_scan(r, decay, k, v, a_vec, b_vec):
    b, s, hh, n = r.shape
    xs = tuple(jnp.moveaxis(t, 1, 0) for t in (r, decay, k, v, a_vec, b_vec))
    state0 = jnp.zeros((b, hh, n, n), jnp.float32)

    def step(state, inp):
        r_t, w_t, k_t, v_t, a_t, b_t = inp
        sa = jnp.einsum('bhvk,bhk->bhv', state, a_t)
        state = (state * w_t[:, :, None, :] + sa[..., None] * b_t[:, :, None, :]
                 + v_t[..., None] * k_t[:, :, None, :])
        y_t = jnp.einsum('bhvk,bhk->bhv', state, r_t)
        return state, y_t

    _, ys = lax.scan(step, state0, xs)
    return jnp.moveaxis(ys, 0, 1)


def heads(z):
    b, s, _ = z.shape
    return z.astype(jnp.float32).reshape(b, s, N_HEADS, HEAD_SIZE)


def setup_inputs(seed: int = 0) -> dict:
    key = jax.random.key(seed)
    ks = jax.random.split(key, 24)
    nrm = lambda kk, shape, sc: sc * jax.random.normal(kk, shape, jnp.float32)
    x = nrm(ks[0], (BATCH, SEQ, D_MODEL), 1.0)
    norm_g = 1.0 + nrm(ks[1], (DEPTH, D_MODEL), 0.05)
    w_in = nrm(ks[2], (DEPTH, D_MODEL, IN_WIDTH), D_MODEL ** -0.5)
    pool_lin = nrm(ks[3], (DEPTH, N_POOL_GROUPS, POOL_GROUP, POOL_GROUP), POOL_GROUP ** -0.5)
    pool_scale = 1.0 + nrm(ks[4], (DEPTH, POOL_WIDTH), 0.1)
    w_pool_proj = nrm(ks[5], (DEPTH, POOL_WIDTH, D_MODEL), POOL_WIDTH ** -0.5)
    mu_shift = jax.random.uniform(ks[6], (DEPTH, SHIFT_WIDTH), jnp.float32)
    decay_w0 = jax.random.uniform(ks[7], (DEPTH, RWKV_WIDTH), jnp.float32, -6.0, -1.0)
    decay_w2 = nrm(ks[8], (DEPTH, D_DECAY_LORA, RWKV_WIDTH), 0.5 * D_DECAY_LORA ** -0.5)
    a0 = nrm(ks[9], (DEPTH, RWKV_WIDTH), 0.5)
    a2 = nrm(ks[10], (DEPTH, D_AAA_LORA, RWKV_WIDTH), 0.5 * D_AAA_LORA ** -0.5)
    vres_v0 = nrm(ks[11], (DEPTH - 1, RWKV_WIDTH), 0.5)
    vres_v1 = nrm(ks[12], (DEPTH - 1, RWKV_WIDTH, D_MV_LORA), RWKV_WIDTH ** -0.5)
    vres_v2 = nrm(ks[13], (DEPTH - 1, D_MV_LORA, RWKV_WIDTH), 0.5 * D_MV_LORA ** -0.5)
    k_k = 0.85 + nrm(ks[14], (DEPTH, RWKV_WIDTH), 0.05)
    k_a = 1.0 + nrm(ks[15], (DEPTH, RWKV_WIDTH), 0.05)
    r_k = nrm(ks[16], (DEPTH, N_HEADS, HEAD_SIZE), 0.1)
    lnx_w = 1.0 + nrm(ks[17], (DEPTH, RWKV_WIDTH), 0.05)
    lnx_b = nrm(ks[18], (DEPTH, RWKV_WIDTH), 0.01)
    w_rwkv_proj = nrm(ks[19], (DEPTH, RWKV_WIDTH, D_MODEL), RWKV_WIDTH ** -0.5)
    w_out = nrm(ks[20], (DEPTH, D_MODEL, D_MODEL), D_MODEL ** -0.5)
    final_g = 1.0 + nrm(ks[21], (D_MODEL,), 0.05)
    return {'x': x, 'norm_g': norm_g, 'w_in': w_in, 'pool_lin': pool_lin, 'pool_scale': pool_scale,
            'w_pool_proj': w_pool_proj, 'mu_shift': mu_shift, 'decay_w0': decay_w0, 'decay_w2': decay_w2,
            'a0': a0, 'a2': a2, 'vres_v0': vres_v0, 'vres_v1': vres_v1, 'vres_v2': vres_v2,
            'k_k': k_k, 'k_a': k_a, 'r_k': r_k, 'lnx_w': lnx_w, 'lnx_b': lnx_b,
            'w_rwkv_proj': w_rwkv_proj, 'w_out': w_out, 'final_g': final_g}


def reference(x, norm_g, w_in, pool_lin, pool_scale, w_pool_proj, mu_shift, decay_w0, decay_w2,
              a0, a2, vres_v0, vres_v1, vres_v2, k_k, k_a, r_k, lnx_w, lnx_b,
              w_rwkv_proj, w_out, final_g):
    b, s, _ = x.shape
    v_first = None
    for i in range(DEPTH):
        h = rms_norm(x, norm_g[i])
        proj = jnp.einsum('bsd,de->bse', h, w_in[i])
        pool_in, pool_gate, rkvwa, rwkv_gate, merge_logits = split_cols(proj, IN_SPLITS)

        y_pool = pool_mixer(pool_in, pool_lin[i], pool_scale[i]) * jax.nn.silu(pool_gate)
        y_pool = jnp.einsum('bsp,pd->bsd', y_pool, w_pool_proj[i])

        rkvwa = token_shift(rkvwa, mu_shift[i])
        r, k, v, w_dn, a_dn = split_cols(
            rkvwa, (RWKV_WIDTH, RWKV_WIDTH, RWKV_WIDTH, D_DECAY_LORA, D_AAA_LORA))
        w_dn = w_dn.astype(jnp.float32)
        decay_logit = -jax.nn.softplus(-(decay_w0[i] + jnp.tanh(w_dn) @ decay_w2[i])) - 0.5
        decay = jnp.exp(-jnp.exp(decay_logit.astype(jnp.float32)))
        a = jax.nn.sigmoid((a0[i] + a_dn @ a2[i]).astype(jnp.float32))
        if i == 0:
            v_first = v
        else:
            mix = jax.nn.sigmoid(vres_v0[i - 1] + (v @ vres_v1[i - 1]) @ vres_v2[i - 1])
            v = v + (v_first - v) * mix
        kk = heads(k * k_k[i])
        kk = kk / jnp.maximum(jnp.linalg.norm(kk, axis=-1, keepdims=True), 1e-12)
        k = k.astype(jnp.float32) * (1.0 + (a - 1.0) * k_a[i])
        rh, kh, vh, ah = heads(r), heads(k), heads(v), heads(a)
        y = rwkv7_scan(rh, heads(decay), kh, vh, -kk, kk * ah)
        mu = jnp.mean(y, axis=-1, keepdims=True)
        var = jnp.mean(jnp.square(y - mu), axis=-1, keepdims=True)
        y = (y - mu) * lax.rsqrt(var + LNX_EPS)
        y = y.reshape(b, s, RWKV_WIDTH) * lnx_w[i] + lnx_b[i]
        bonus = jnp.sum(rh * kh * r_k[i], axis=-1, keepdims=True) * vh
        y = (y + bonus.reshape(b, s, RWKV_WIDTH)).astype(x.dtype)
        y_rwkv = jnp.einsum('bsc,cd->bsd', y * jax.nn.silu(rwkv_gate), w_rwkv_proj[i])

        g_pool, g_rwkv = split_cols(merge_logits, (D_MODEL, D_MODEL))
        merged = jax.nn.sigmoid(g_pool) * y_pool + jax.nn.sigmoid(g_rwkv) * y_rwkv
        x = x + jnp.einsum('bsd,de->bse', merged, w_out[i])
    return rms_norm(x, final_g)
```

```python
import functools

import numpy as np
import jax
import jax.numpy as jnp
from jax import lax
from jax.experimental import pallas as pl
from jax.experimental.pallas import tpu as pltpu

F32 = jnp.float32
BF16 = jnp.bfloat16

D_MODEL = 1024
DEPTH = 4
POOL_WINDOWS = (2, 4, 8, 16)
POOL_GROUP = D_MODEL // len(POOL_WINDOWS)
POOL_HALO = max(POOL_WINDOWS)
HEAD_SIZE = 64
D_LORA = 64
D_MV_LORA = 32
LANE = 128
NORM_EPS = 1e-6
LNX_EPS = 1e-5 * HEAD_SIZE

OFF_POOL_IN = 0
OFF_POOL_GATE = D_MODEL
OFF_R = 2 * D_MODEL
OFF_K = 3 * D_MODEL
OFF_V = 4 * D_MODEL
OFF_LORA = 5 * D_MODEL
OFF_RWKV_GATE = OFF_LORA + 2 * D_LORA
OFF_G_POOL = OFF_RWKV_GATE + D_MODEL
OFF_G_RWKV = OFF_G_POOL + D_MODEL
IN_WIDTH = OFF_G_RWKV + D_MODEL
SHIFT_WIDTH = 3 * D_MODEL + 2 * D_LORA

MXU_TILE = 256
GROUP_HEADS = MXU_TILE // HEAD_SIZE
GROUP_W = MXU_TILE
N_GROUPS = D_MODEL // GROUP_W
CHUNK = 64
SEQ_TILE = 256
VMEM_LIMIT_BYTES = 56 * 1024 * 1024


def _sigmoid(z):
    return 1.0 / (1.0 + jnp.exp(-z))


def _silu(z):
    return z * _sigmoid(z)


def _dot(a, b):
    return jnp.dot(a, b, preferred_element_type=F32)


def _dot_nt(a, b):
    return lax.dot_general(a, b, (((1,), (1,)), ((), ())), preferred_element_type=F32)


def _dot_tn(a, b):
    return lax.dot_general(a, b, (((0,), (0,)), ((), ())), preferred_element_type=F32)


def _split_bf16(z):
    hi = z.astype(BF16)
    lo = (z - hi.astype(F32)).astype(BF16)
    return hi, lo


def _head_sums(z, ones2_ref):
    outs = []
    for g in range(N_GROUPS):
        hi, lo = _split_bf16(z[:, g * GROUP_W:(g + 1) * GROUP_W])
        outs.append(_dot(jnp.concatenate([hi, lo], axis=1), ones2_ref[...]))
    return jnp.concatenate(outs, axis=1)


def _tile_rows(z):
    return jnp.concatenate([z] * GROUP_HEADS, axis=0)


def _scan_group_chunk(r, lw, cum, k, v, kk, a, state, consts):
    blk_bf, blk_f32, m_s_bd, m_s_cat, m_i_cat, eye = consts
    cum_last = cum[CHUNK - 1:CHUNK, :]
    e_prev = jnp.exp(cum - lw)
    e_neg = jnp.exp(-cum)
    e_pos = jnp.exp(cum)
    e_tail = jnp.exp(cum_last - cum)
    kka = kk * a
    a_t = (-kk * e_prev).astype(BF16)
    r_t = (r * e_pos).astype(BF16)
    b_t = (kka * e_neg).astype(BF16)
    k_t = (k * e_neg).astype(BF16)
    b_end = kka * e_tail
    k_end = k * e_tail
    v_bf = v.astype(BF16)

    def tile_mask(z):
        return _tile_rows(z) * blk_bf

    lhs = jnp.concatenate([a_t, r_t], axis=0)
    rhs = jnp.concatenate([tile_mask(b_t), tile_mask(k_t)], axis=0)
    a_all = _dot_nt(lhs, rhs)
    a_ab = a_all[:CHUNK, :GROUP_W]
    a_ak = a_all[:CHUNK, GROUP_W:] * m_s_cat
    a_rb = a_all[CHUNK:, :GROUP_W] * m_i_cat
    a_rk = a_all[CHUNK:, GROUP_W:] * m_i_cat
    a_bd = _tile_rows(a_ab) * m_s_bd

    x_acc = eye + a_bd
    p_bf = a_bd.astype(BF16)
    p_bf = _dot(p_bf, p_bf).astype(BF16)
    n_levels = CHUNK.bit_length() - 1
    for level in range(1, n_levels):
        if level < n_levels - 1:
            both = _dot(jnp.concatenate([x_acc.astype(BF16), p_bf], axis=0), p_bf)
            x_acc = x_acc + both[:GROUP_W]
            p_bf = both[GROUP_W:].astype(BF16)
        else:
            x_acc = x_acc + _dot(x_acc.astype(BF16), p_bf)
    t_cat = x_acc[0:CHUNK]
    for hh in range(1, GROUP_HEADS):
        t_cat = t_cat + x_acc[hh * CHUNK:(hh + 1) * CHUNK]

    s_bf = state.astype(BF16)
    v_bd = tile_mask(v_bf)
    from_state = _dot_nt(lhs, s_bf)
    from_v = _dot(jnp.concatenate([a_ak, a_rk], axis=0).astype(BF16), v_bd)
    w_cat = from_state[:CHUNK] + from_v[:CHUNK]
    u_cat = _dot(t_cat.astype(BF16), tile_mask(w_cat.astype(BF16)))
    u_bf = u_cat.astype(BF16)
    y = from_state[CHUNK:] + from_v[CHUNK:] + _dot(a_rb.astype(BF16), tile_mask(u_bf))
    upd = _dot_tn(jnp.concatenate([u_bf, v_bf], axis=0),
                  jnp.concatenate([b_end, k_end], axis=0).astype(BF16))
    new_state = state * jnp.exp(cum_last) + upd * blk_f32
    return y, new_state


def _layer_kernel(*refs, first, last):
    it = iter(refs)
    x_ref = next(it)
    vfirst_ref = None if first else next(it)
    norm_g_ref = next(it)
    w_in_ref = next(it)
    pool_lin_ref = next(it)
    pool_scale_ref = next(it)
    w_pool_proj_ref = next(it)
    mu_ref = next(it)
    w0_ref = next(it)
    w2_ref = next(it)
    a0_ref = next(it)
    a2_ref = next(it)
    if not first:
        v0_ref = next(it)
        v1_ref = next(it)
        v2_ref = next(it)
    kk_scale_ref = next(it)
    ka_ref = next(it)
    rk_ref = next(it)
    lnx_w_ref = next(it)
    lnx_b_ref = next(it)
    w_rwkv_ref = next(it)
    w_out_ref = next(it)
    final_g_ref = next(it) if last else None
    blk_bf_ref = next(it)
    ones2_ref = next(it)
    m_s_bd_ref = next(it)
    m_s_cat_ref = next(it)
    m_i_cat_ref = next(it)
    eye_ref = next(it)
    tri2_ref = next(it)
    out_ref = next(it)
    vfirst_out_ref = next(it) if first else None
    state_ref = next(it)
    pool_halo_ref = next(it)
    shift_halo_ref = next(it)
    r_s, lw_s, k_s, v_s, kk_s, a_s, y_s = (next(it) for _ in range(7))

    ts = x_ref.shape[1]
    s_idx = pl.program_id(1)

    @pl.when(s_idx == 0)
    def _():
        state_ref[...] = jnp.zeros_like(state_ref)
        pool_halo_ref[...] = jnp.zeros_like(pool_halo_ref)
        shift_halo_ref[...] = jnp.zeros_like(shift_halo_ref)

    x = x_ref[0]
    h = (x * lax.rsqrt(jnp.mean(x * x, axis=-1, keepdims=True) + NORM_EPS) * norm_g_ref[...]).astype(BF16)

    def proj(off, width):
        return _dot(h, w_in_ref[:, off:off + width])

    row = lax.broadcasted_iota(jnp.int32, (ts, 1), 0)

    u = proj(OFF_POOL_IN, D_MODEL)
    ext = jnp.concatenate([pool_halo_ref[...], u], axis=0)
    pool_halo_ref[...] = u[ts - POOL_HALO:, :]
    pos = (s_idx * ts + row + 1).astype(F32)
    mixed = []
    for gi, win in enumerate(POOL_WINDOWS):
        sl = slice(gi * POOL_GROUP, (gi + 1) * POOL_GROUP)
        acc = ext[:, sl]
        step = 1
        while step < win:
            acc = acc + pltpu.roll(acc, step, axis=0)
            step *= 2
        pooled = acc[POOL_HALO:, :] * (1.0 / jnp.minimum(pos, float(win)))
        mixed.append(_dot((pooled - u[:, sl]).astype(BF16), pool_lin_ref[gi]))
    mixed = jnp.concatenate(mixed, axis=1) * pool_scale_ref[...]
    y_pool = _dot((mixed * _silu(proj(OFF_POOL_GATE, D_MODEL))).astype(BF16), w_pool_proj_ref[...])

    def token_shift(z, off, width):
        prev = pltpu.roll(z, 1, axis=0)
        prev = jnp.where(row == 0, shift_halo_ref[0:1, off:off + width], prev)
        shift_halo_ref[0:1, off:off + width] = z[ts - 1:ts, :]
        return z + mu_ref[:, off:off + width] * (prev - z)

    r = token_shift(proj(OFF_R, D_MODEL), 0, D_MODEL)
    k = token_shift(proj(OFF_K, D_MODEL), D_MODEL, D_MODEL)
    v = token_shift(proj(OFF_V, D_MODEL), 2 * D_MODEL, D_MODEL)
    lora = token_shift(proj(OFF_LORA, 2 * D_LORA), 3 * D_MODEL, 2 * D_LORA)
    lw = -float(np.exp(-0.5)) * _sigmoid(w0_ref[...] + _dot(jnp.tanh(lora).astype(BF16), w2_ref[...]))
    a = _sigmoid(a0_ref[...] + _dot(lora.astype(BF16), a2_ref[...]))
    if first:
        vfirst_out_ref[0] = v
    else:
        low = _dot(v.astype(BF16), v1_ref[...])
        mix = _sigmoid(v0_ref[...] + _dot(low.astype(BF16), v2_ref[...]))
        v = v + (vfirst_ref[0] - v) * mix
    kk = k * kk_scale_ref[...]
    kk = kk * lax.rsqrt(jnp.maximum(_head_sums(kk * kk, ones2_ref), 1e-24))
    k = k * (1.0 + (a - 1.0) * ka_ref[...])
    r_s[...] = r
    lw_s[...] = lw
    k_s[...] = k
    v_s[...] = v
    kk_s[...] = kk
    a_s[...] = a

    consts = (blk_bf_ref[...], blk_bf_ref[...].astype(F32), m_s_bd_ref[...], m_s_cat_ref[...],
              m_i_cat_ref[...], eye_ref[...])

    def chunk_body(c, carry):
        rows = pl.ds(pl.multiple_of(c * CHUNK, CHUNK), CHUNK)
        lw_c = lw_s[rows, :]
        hi, lo = _split_bf16(lw_c)
        cum = _dot(tri2_ref[...], jnp.concatenate([hi, lo], axis=0))
        for g in range(N_GROUPS):
            ls = slice(g * GROUP_W, (g + 1) * GROUP_W)
            y, new_state = _scan_group_chunk(
                r_s[rows, ls], lw_c[:, ls], cum[:, ls], k_s[rows, ls], v_s[rows, ls], kk_s[rows, ls],
                a_s[rows, ls], state_ref[g], consts)
            y_s[rows, ls] = y
            state_ref[g] = new_state
        return carry

    lax.fori_loop(0, ts // CHUNK, chunk_body, 0)

    y = y_s[...]
    inv_n = 1.0 / HEAD_SIZE
    dev = y - _head_sums(y, ones2_ref) * inv_n
    var = _head_sums(dev * dev, ones2_ref) * inv_n
    y = dev * lax.rsqrt(var + LNX_EPS) * lnx_w_ref[...] + lnx_b_ref[...]
    y = y + _head_sums(r_s[...] * k_s[...] * rk_ref[...], ones2_ref) * v_s[...]
    y_rwkv = _dot((y * _silu(proj(OFF_RWKV_GATE, D_MODEL))).astype(BF16), w_rwkv_ref[...])

    merged = (_sigmoid(proj(OFF_G_POOL, D_MODEL)) * y_pool
              + _sigmoid(proj(OFF_G_RWKV, D_MODEL)) * y_rwkv)
    xo = x + _dot(merged.astype(BF16), w_out_ref[...])
    if last:
        xo = xo * lax.rsqrt(jnp.mean(xo * xo, axis=-1, keepdims=True) + NORM_EPS) * final_g_ref[...]
    out_ref[0] = xo


def _scan_constants():
    idx = np.arange(GROUP_W)
    same_head = (idx[:, None] // HEAD_SIZE) == (idx[None, :] // HEAD_SIZE)
    t_row = idx[:, None] % CHUNK
    t_col = idx[None, :] % CHUNK
    t64 = np.arange(CHUNK)[:, None]
    blk = same_head.astype(np.float32)
    m_s_bd = (same_head & (t_col < t_row)).astype(np.float32)
    m_s_cat = (t_col < t64).astype(np.float32)
    m_i_cat = (t_col <= t64).astype(np.float32)
    tri = (np.arange(CHUNK)[None, :] <= t64).astype(np.float32)
    return (jnp.asarray(blk, BF16), jnp.asarray(np.concatenate([blk, blk], axis=0), BF16),
            jnp.asarray(m_s_bd), jnp.asarray(m_s_cat), jnp.asarray(m_i_cat),
            jnp.asarray(np.eye(GROUP_W, dtype=np.float32)),
            jnp.asarray(np.concatenate([tri, tri], axis=1), BF16))


def _resident(arr):
    nd = arr.ndim
    return pl.BlockSpec(arr.shape, lambda b, s, _nd=nd: (0,) * _nd, pipeline_mode=pl.Buffered(1))


def _layer_call(x, vfirst, weights, consts, *, first, last):
    batch, seq, _ = x.shape
    ts = SEQ_TILE
    assert seq % ts == 0 and ts % CHUNK == 0 and ts >= POOL_HALO
    tile = pl.BlockSpec((1, ts, D_MODEL), lambda b, s: (b, s, 0))
    operands = [x] + ([] if first else [vfirst]) + list(weights) + list(consts)
    in_specs = [tile] + ([] if first else [tile]) + [_resident(w) for w in list(weights) + list(consts)]
    act = jax.ShapeDtypeStruct(x.shape, F32)
    out_shape = (act, act) if first else act
    out_specs = (tile, tile) if first else tile
    scratch = [
        pltpu.VMEM((N_GROUPS, GROUP_W, GROUP_W), F32),
        pltpu.VMEM((POOL_HALO, D_MODEL), F32),
        pltpu.VMEM((8, SHIFT_WIDTH), F32),
    ] + [pltpu.VMEM((ts, D_MODEL), F32) for _ in range(7)]
    return pl.pallas_call(
        functools.partial(_layer_kernel, first=first, last=last),
        grid=(batch, seq // ts),
        in_specs=in_specs,
        out_specs=out_specs,
        out_shape=out_shape,
        scratch_shapes=scratch,
        compiler_params=pltpu.CompilerParams(
            dimension_semantics=("arbitrary", "arbitrary"),
            vmem_limit_bytes=VMEM_LIMIT_BYTES),
        name="rwkv_pool_layer",
    )(*operands)


def _pad_rows(w, rows, offset):
    out = jnp.zeros((rows, w.shape[1]), w.dtype)
    return out.at[offset:offset + w.shape[0]].set(w)


def kernel(x, norm_g, w_in, pool_lin, pool_scale, w_pool_proj, mu_shift, decay_w0, decay_w2, a0, a2,
           vres_v0, vres_v1, vres_v2, k_k, k_a, r_k, lnx_w, lnx_b, w_rwkv_proj, w_out, final_g):
    consts = _scan_constants()
    row = lambda p: p.reshape(1, -1).astype(F32)
    vfirst = None
    for i in range(DEPTH):
        first, last = i == 0, i == DEPTH - 1
        weights = [row(norm_g[i]), w_in[i].astype(BF16), pool_lin[i].astype(BF16), row(pool_scale[i]),
                   w_pool_proj[i].astype(BF16), row(mu_shift[i]), row(decay_w0[i]),
                   _pad_rows(decay_w2[i], 2 * D_LORA, 0).astype(BF16), row(a0[i]),
                   _pad_rows(a2[i], 2 * D_LORA, D_LORA).astype(BF16)]
        if not first:
            v1 = jnp.zeros((D_MODEL, LANE), F32).at[:, :D_MV_LORA].set(vres_v1[i - 1])
            weights += [row(vres_v0[i - 1]), v1.astype(BF16),
                        _pad_rows(vres_v2[i - 1], LANE, 0).astype(BF16)]
        weights += [row(k_k[i]), row(k_a[i]), row(r_k[i]), row(lnx_w[i]), row(lnx_b[i]),
                    w_rwkv_proj[i].astype(BF16), w_out[i].astype(BF16)]
        if last:
            weights.append(row(final_g))
        res = _layer_call(x, vfirst, weights, consts, first=first, last=last)
        if first:
            x, vfirst = res
        else:
            x = res
    return x
```

```python
import functools

import numpy as np
import jax
import jax.numpy as jnp
from jax import lax
from jax.experimental import pallas as pl
from jax.experimental.pallas import tpu as pltpu

F32 = jnp.float32
BF16 = jnp.bfloat16

D_MODEL = 1024
DEPTH = 4
POOL_WINDOWS = (2, 4, 8, 16)
POOL_GROUP = D_MODEL // len(POOL_WINDOWS)
POOL_HALO = max(POOL_WINDOWS)
HEAD_SIZE = 64
D_LORA = 64
D_MV_LORA = 32
LANE = 128
NORM_EPS = 1e-6
LNX_EPS = 1e-5 * HEAD_SIZE

OFF_POOL_IN = 0
OFF_POOL_GATE = D_MODEL
OFF_R = 2 * D_MODEL
OFF_K = 3 * D_MODEL
OFF_V = 4 * D_MODEL
OFF_LORA = 5 * D_MODEL
OFF_RWKV_GATE = OFF_LORA + 2 * D_LORA
OFF_G_POOL = OFF_RWKV_GATE + D_MODEL
OFF_G_RWKV = OFF_G_POOL + D_MODEL
IN_WIDTH = OFF_G_RWKV + D_MODEL
SHIFT_WIDTH = 3 * D_MODEL + 2 * D_LORA

MXU_TILE = 256
GROUP_HEADS = MXU_TILE // HEAD_SIZE
GROUP_W = MXU_TILE
N_GROUPS = D_MODEL // GROUP_W
CHUNK = 64
SEQ_TILE = 256
VMEM_LIMIT_BYTES = 56 * 1024 * 1024


def _sigmoid(z):
    return 1.0 / (1.0 + jnp.exp(-z))


def _silu(z):
    return z * _sigmoid(z)


def _dot(a, b):
    return jnp.dot(a, b, preferred_element_type=F32)


def _dot_nt(a, b):
    return lax.dot_general(a, b, (((1,), (1,)), ((), ())), preferred_element_type=F32)


def _dot_tn(a, b):
    return lax.dot_general(a, b, (((0,), (0,)), ((), ())), preferred_element_type=F32)


def _split_bf16(z):
    hi = z.astype(BF16)
    lo = (z - hi.astype(F32)).astype(BF16)
    return hi, lo


def _head_sums(z, ones2_ref):
    outs = []
    for g in range(N_GROUPS):
        hi, lo = _split_bf16(z[:, g * GROUP_W:(g + 1) * GROUP_W])
        outs.append(_dot(jnp.concatenate([hi, lo], axis=1), ones2_ref[...]))
    return jnp.concatenate(outs, axis=1)


def _tile_rows(z):
    return jnp.concatenate([z] * GROUP_HEADS, axis=0)


def _tile_mask(z, blk_bf):
    return _tile_rows(z) * blk_bf


def _chunk_rows(c):
    return pl.ds(pl.multiple_of(c * CHUNK, CHUNK), CHUNK)


_GROUP_LANES = [slice(g * GROUP_W, (g + 1) * GROUP_W) for g in range(N_GROUPS)]


def _chunk_inverse_stage(c, refs, consts):
    at_s, rt_s, bt_s, kt_s, vbf_s, tcat_s, arb_s, fvw_s, fvy_s = refs
    blk_bf, m_s_bd, m_s_cat, m_i_cat, eye = consts
    rows = _chunk_rows(c)
    groups = range(N_GROUPS)
    lanes = _GROUP_LANES
    lhs = [jnp.concatenate([at_s[rows, ls], rt_s[rows, ls]], axis=0) for ls in lanes]
    rhs = [jnp.concatenate([_tile_mask(bt_s[rows, ls], blk_bf), _tile_mask(kt_s[rows, ls], blk_bf)], axis=0)
           for ls in lanes]
    a_all = [_dot_nt(lhs[g], rhs[g]) for g in groups]
    a_bd = [_tile_rows(a_all[g][:CHUNK, :GROUP_W]) * m_s_bd for g in groups]
    x_acc = [eye + a_bd[g] for g in groups]
    p_bf = [a_bd[g].astype(BF16) for g in groups]
    p_bf = [_dot(p_bf[g], p_bf[g]).astype(BF16) for g in groups]
    n_levels = CHUNK.bit_length() - 1
    for level in range(1, n_levels):
        if level < n_levels - 1:
            both = [_dot(jnp.concatenate([x_acc[g].astype(BF16), p_bf[g]], axis=0), p_bf[g]) for g in groups]
            x_acc = [x_acc[g] + both[g][:GROUP_W] for g in groups]
            p_bf = [both[g][GROUP_W:].astype(BF16) for g in groups]
        else:
            x_acc = [x_acc[g] + _dot(x_acc[g].astype(BF16), p_bf[g]) for g in groups]
    for g in groups:
        t_cat = x_acc[g][0:CHUNK]
        for hh in range(1, GROUP_HEADS):
            t_cat = t_cat + x_acc[g][hh * CHUNK:(hh + 1) * CHUNK]
        tcat_s[rows, lanes[g]] = t_cat.astype(BF16)
        arb_s[rows, lanes[g]] = (a_all[g][CHUNK:, :GROUP_W] * m_i_cat).astype(BF16)
    from_v = [_dot(jnp.concatenate([a_all[g][:CHUNK, GROUP_W:] * m_s_cat, a_all[g][CHUNK:, GROUP_W:] * m_i_cat],
                                   axis=0).astype(BF16), _tile_mask(vbf_s[rows, lanes[g]], blk_bf))
              for g in groups]
    for g in groups:
        fvw_s[rows, lanes[g]] = from_v[g][:CHUNK]
        fvy_s[rows, lanes[g]] = from_v[g][CHUNK:]


def _chunk_state_stage(c, refs, consts):
    at_s, rt_s, vbf_s, tcat_s, arb_s, fvw_s, fvy_s, bend_s, kend_s, plast_s, y_s, state_ref = refs
    blk_bf, blk_f32 = consts
    rows = _chunk_rows(c)
    first_row = pl.ds(pl.multiple_of(c * CHUNK, CHUNK), 1)
    groups = range(N_GROUPS)
    lanes = _GROUP_LANES
    state = [state_ref[g] for g in groups]
    lhs = [jnp.concatenate([at_s[rows, ls], rt_s[rows, ls]], axis=0) for ls in lanes]
    from_state = [_dot_nt(lhs[g], state[g].astype(BF16)) for g in groups]
    w_bd = [_tile_mask((from_state[g][:CHUNK] + fvw_s[rows, lanes[g]]).astype(BF16), blk_bf) for g in groups]
    u_bf = [_dot(tcat_s[rows, lanes[g]], w_bd[g]).astype(BF16) for g in groups]
    y_u = [_dot(arb_s[rows, lanes[g]], _tile_mask(u_bf[g], blk_bf)) for g in groups]
    upd = [_dot_tn(jnp.concatenate([u_bf[g], vbf_s[rows, lanes[g]]], axis=0),
                   jnp.concatenate([bend_s[rows, lanes[g]], kend_s[rows, lanes[g]]], axis=0)) for g in groups]
    for g in groups:
        y_s[rows, lanes[g]] = from_state[g][CHUNK:] + fvy_s[rows, lanes[g]] + y_u[g]
        state_ref[g] = state[g] * plast_s[first_row, lanes[g]] + upd[g] * blk_f32


def _layer_kernel(*refs, first, last):
    it = iter(refs)
    x_ref = next(it)
    vfirst_ref = None if first else next(it)
    norm_g_ref = next(it)
    w_in_ref = next(it)
    pool_lin_ref = next(it)
    pool_scale_ref = next(it)
    w_pool_proj_ref = next(it)
    mu_ref = next(it)
    w0_ref = next(it)
    w2_ref = next(it)
    a0_ref = next(it)
    a2_ref = next(it)
    if not first:
        v0_ref = next(it)
        v1_ref = next(it)
        v2_ref = next(it)
    kk_scale_ref = next(it)
    ka_ref = next(it)
    rk_ref = next(it)
    lnx_w_ref = next(it)
    lnx_b_ref = next(it)
    w_rwkv_ref = next(it)
    w_out_ref = next(it)
    final_g_ref = next(it) if last else None
    blk_bf_ref = next(it)
    ones2_ref = next(it)
    m_s_bd_ref = next(it)
    m_s_cat_ref = next(it)
    m_i_cat_ref = next(it)
    eye_ref = next(it)
    cumsum_ref = next(it)
    out_ref = next(it)
    vfirst_out_ref = next(it) if first else None
    state_ref = next(it)
    pool_halo_ref = next(it)
    shift_halo_ref = next(it)
    at_s, rt_s, bt_s, kt_s, bend_s, kend_s, vbf_s, tcat_s, arb_s = (next(it) for _ in range(9))
    plast_s, fvw_s, fvy_s, y_s, bonus_s = (next(it) for _ in range(5))

    ts = x_ref.shape[1]
    s_idx = pl.program_id(1)

    @pl.when(s_idx == 0)
    def _():
        state_ref[...] = jnp.zeros_like(state_ref)
        pool_halo_ref[...] = jnp.zeros_like(pool_halo_ref)
        shift_halo_ref[...] = jnp.zeros_like(shift_halo_ref)

    x = x_ref[0]
    h = (x * lax.rsqrt(jnp.mean(x * x, axis=-1, keepdims=True) + NORM_EPS) * norm_g_ref[...]).astype(BF16)

    def proj(off, width):
        return _dot(h, w_in_ref[:, off:off + width])

    row = lax.broadcasted_iota(jnp.int32, (ts, 1), 0)

    u = proj(OFF_POOL_IN, D_MODEL)
    ext = jnp.concatenate([pool_halo_ref[...], u], axis=0)
    pool_halo_ref[...] = u[ts - POOL_HALO:, :]
    pos = (s_idx * ts + row + 1).astype(F32)
    mixed = []
    for gi, win in enumerate(POOL_WINDOWS):
        sl = slice(gi * POOL_GROUP, (gi + 1) * POOL_GROUP)
        acc = ext[:, sl]
        step = 1
        while step < win:
            acc = acc + pltpu.roll(acc, step, axis=0)
            step *= 2
        pooled = acc[POOL_HALO:, :] * (1.0 / jnp.minimum(pos, float(win)))
        mixed.append(_dot((pooled - u[:, sl]).astype(BF16), pool_lin_ref[gi]))
    mixed = jnp.concatenate(mixed, axis=1) * pool_scale_ref[...]
    y_pool = _dot((mixed * _silu(proj(OFF_POOL_GATE, D_MODEL))).astype(BF16), w_pool_proj_ref[...])

    def token_shift(z, off, width):
        prev = pltpu.roll(z, 1, axis=0)
        prev = jnp.where(row == 0, shift_halo_ref[0:1, off:off + width], prev)
        shift_halo_ref[0:1, off:off + width] = z[ts - 1:ts, :]
        return z + mu_ref[:, off:off + width] * (prev - z)

    r = token_shift(proj(OFF_R, D_MODEL), 0, D_MODEL)
    k = token_shift(proj(OFF_K, D_MODEL), D_MODEL, D_MODEL)
    v = token_shift(proj(OFF_V, D_MODEL), 2 * D_MODEL, D_MODEL)
    lora = token_shift(proj(OFF_LORA, 2 * D_LORA), 3 * D_MODEL, 2 * D_LORA)
    lw = -float(np.exp(-0.5)) * _sigmoid(w0_ref[...] + _dot(jnp.tanh(lora).astype(BF16), w2_ref[...]))
    a = _sigmoid(a0_ref[...] + _dot(lora.astype(BF16), a2_ref[...]))
    if first:
        vfirst_out_ref[0] = v
    else:
        low = _dot(v.astype(BF16), v1_ref[...])
        mix = _sigmoid(v0_ref[...] + _dot(low.astype(BF16), v2_ref[...]))
        v = v + (vfirst_ref[0] - v) * mix
    kk = k * kk_scale_ref[...]
    kk = kk * lax.rsqrt(jnp.maximum(_head_sums(kk * kk, ones2_ref), 1e-24))
    k = k * (1.0 + (a - 1.0) * ka_ref[...])
    bonus_s[...] = _head_sums(r * k * rk_ref[...], ones2_ref) * v

    hi, lo = _split_bf16(lw)
    sums = _dot(cumsum_ref[...], jnp.concatenate([hi, lo], axis=0))
    cum, total = sums[:ts], sums[ts:]
    e_neg = jnp.exp(-cum)
    e_tail = jnp.exp(total - cum)
    kka = kk * a
    at_s[...] = (-kk * jnp.exp(cum - lw)).astype(BF16)
    rt_s[...] = (r * jnp.exp(cum)).astype(BF16)
    bt_s[...] = (kka * e_neg).astype(BF16)
    kt_s[...] = (k * e_neg).astype(BF16)
    bend_s[...] = (kka * e_tail).astype(BF16)
    kend_s[...] = (k * e_tail).astype(BF16)
    vbf_s[...] = v.astype(BF16)
    plast_s[...] = jnp.exp(total)

    blk_bf = blk_bf_ref[...]
    inv_refs = (at_s, rt_s, bt_s, kt_s, vbf_s, tcat_s, arb_s, fvw_s, fvy_s)
    inv_consts = (blk_bf, m_s_bd_ref[...], m_s_cat_ref[...], m_i_cat_ref[...], eye_ref[...])

    def inverse_body(c, carry):
        _chunk_inverse_stage(c, inv_refs, inv_consts)
        return carry

    lax.fori_loop(0, ts // CHUNK, inverse_body, 0)

    st_refs = (at_s, rt_s, vbf_s, tcat_s, arb_s, fvw_s, fvy_s, bend_s, kend_s, plast_s, y_s, state_ref)
    st_consts = (blk_bf, blk_bf.astype(F32))

    def state_body(c, carry):
        _chunk_state_stage(c, st_refs, st_consts)
        return carry

    lax.fori_loop(0, ts // CHUNK, state_body, 0)

    y = y_s[...]
    inv_n = 1.0 / HEAD_SIZE
    dev = y - _head_sums(y, ones2_ref) * inv_n
    var = _head_sums(dev * dev, ones2_ref) * inv_n
    y = dev * lax.rsqrt(var + LNX_EPS) * lnx_w_ref[...] + lnx_b_ref[...] + bonus_s[...]
    y_rwkv = _dot((y * _silu(proj(OFF_RWKV_GATE, D_MODEL))).astype(BF16), w_rwkv_ref[...])

    merged = (_sigmoid(proj(OFF_G_POOL, D_MODEL)) * y_pool
              + _sigmoid(proj(OFF_G_RWKV, D_MODEL)) * y_rwkv)
    xo = x + _dot(merged.astype(BF16), w_out_ref[...])
    if last:
        xo = xo * lax.rsqrt(jnp.mean(xo * xo, axis=-1, keepdims=True) + NORM_EPS) * final_g_ref[...]
    out_ref[0] = xo


def _scan_constants(ts):
    idx = np.arange(GROUP_W)
    same_head = (idx[:, None] // HEAD_SIZE) == (idx[None, :] // HEAD_SIZE)
    t_row = idx[:, None] % CHUNK
    t_col = idx[None, :] % CHUNK
    t64 = np.arange(CHUNK)[:, None]
    blk = same_head.astype(np.float32)
    m_s_bd = (same_head & (t_col < t_row)).astype(np.float32)
    m_s_cat = (t_col < t64).astype(np.float32)
    m_i_cat = (t_col <= t64).astype(np.float32)
    tok = np.arange(ts)
    same_chunk = (tok[:, None] // CHUNK) == (tok[None, :] // CHUNK)
    tri = (same_chunk & (tok[None, :] <= tok[:, None])).astype(np.float32)
    tot = same_chunk.astype(np.float32)
    cumsum = np.concatenate([np.concatenate([tri, tri], axis=1), np.concatenate([tot, tot], axis=1)], axis=0)
    return (jnp.asarray(blk, BF16), jnp.asarray(np.concatenate([blk, blk], axis=0), BF16),
            jnp.asarray(m_s_bd), jnp.asarray(m_s_cat), jnp.asarray(m_i_cat),
            jnp.asarray(np.eye(GROUP_W, dtype=np.float32)), jnp.asarray(cumsum, BF16))


def _resident(arr):
    nd = arr.ndim
    return pl.BlockSpec(arr.shape, lambda b, s, _nd=nd: (0,) * _nd, pipeline_mode=pl.Buffered(1))


def _layer_call(x, vfirst, weights, consts, *, first, last):
    batch, seq, _ = x.shape
    ts = SEQ_TILE
    assert seq % ts == 0 and ts % CHUNK == 0 and ts >= POOL_HALO
    tile = pl.BlockSpec((1, ts, D_MODEL), lambda b, s: (b, s, 0))
    operands = [x] + ([] if first else [vfirst]) + list(weights) + list(consts)
    in_specs = [tile] + ([] if first else [tile]) + [_resident(w) for w in list(weights) + list(consts)]
    act = jax.ShapeDtypeStruct(x.shape, F32)
    out_shape = (act, act) if first else act
    out_specs = (tile, tile) if first else tile
    scratch = [
        pltpu.VMEM((N_GROUPS, GROUP_W, GROUP_W), F32),
        pltpu.VMEM((POOL_HALO, D_MODEL), F32),
        pltpu.VMEM((8, SHIFT_WIDTH), F32),
    ]
    scratch += [pltpu.VMEM((ts, D_MODEL), BF16) for _ in range(9)]
    scratch += [pltpu.VMEM((ts, D_MODEL), F32) for _ in range(5)]
    return pl.pallas_call(
        functools.partial(_layer_kernel, first=first, last=last),
        grid=(batch, seq // ts),
        in_specs=in_specs,
        out_specs=out_specs,
        out_shape=out_shape,
        scratch_shapes=scratch,
        compiler_params=pltpu.CompilerParams(
            dimension_semantics=("arbitrary", "arbitrary"),
            vmem_limit_bytes=VMEM_LIMIT_BYTES),
        name="rwkv_pool_layer",
    )(*operands)


def _pad_rows(w, rows, offset):
    out = jnp.zeros((rows, w.shape[1]), w.dtype)
    return out.at[offset:offset + w.shape[0]].set(w)


def kernel(x, norm_g, w_in, pool_lin, pool_scale, w_pool_proj, mu_shift, decay_w0, decay_w2, a0, a2,
           vres_v0, vres_v1, vres_v2, k_k, k_a, r_k, lnx_w, lnx_b, w_rwkv_proj, w_out, final_g):
    consts = _scan_constants(SEQ_TILE)
    row = lambda p: p.reshape(1, -1).astype(F32)
    vfirst = None
    for i in range(DEPTH):
        first, last = i == 0, i == DEPTH - 1
        weights = [row(norm_g[i]), w_in[i].astype(BF16), pool_lin[i].astype(BF16), row(pool_scale[i]),
                   w_pool_proj[i].astype(BF16), row(mu_shift[i]), row(decay_w0[i]),
                   _pad_rows(decay_w2[i], 2 * D_LORA, 0).astype(BF16), row(a0[i]),
                   _pad_rows(a2[i], 2 * D_LORA, D_LORA).astype(BF16)]
        if not first:
            v1 = jnp.zeros((D_MODEL, LANE), F32).at[:, :D_MV_LORA].set(vres_v1[i - 1])
            weights += [row(vres_v0[i - 1]), v1.astype(BF16),
                        _pad_rows(vres_v2[i - 1], LANE, 0).astype(BF16)]
        weights += [row(k_k[i]), row(k_a[i]), row(r_k[i]), row(lnx_w[i]), row(lnx_b[i]),
                    w_rwkv_proj[i].astype(BF16), w_out[i].astype(BF16)]
        if last:
            weights.append(row(final_g))
        res = _layer_call(x, vfirst, weights, consts, first=first, last=last)
        if first:
            x, vfirst = res
        else:
            x = res
    return x
```

```python
import functools

import numpy as np
import jax
import jax.numpy as jnp
from jax import lax
from jax.experimental import pallas as pl
from jax.experimental.pallas import tpu as pltpu

F32 = jnp.float32
BF16 = jnp.bfloat16

D_MODEL = 1024
DEPTH = 4
POOL_WINDOWS = (2, 4, 8, 16)
POOL_GROUP = D_MODEL // len(POOL_WINDOWS)
POOL_HALO = max(POOL_WINDOWS)
HEAD_SIZE = 64
D_LORA = 64
D_MV_LORA = 32
LANE = 128
NORM_EPS = 1e-6
LNX_EPS = 1e-5 * HEAD_SIZE

OFF_POOL_IN = 0
OFF_POOL_GATE = D_MODEL
OFF_R = 2 * D_MODEL
OFF_K = 3 * D_MODEL
OFF_V = 4 * D_MODEL
OFF_LORA = 5 * D_MODEL
OFF_RWKV_GATE = OFF_LORA + 2 * D_LORA
OFF_G_POOL = OFF_RWKV_GATE + D_MODEL
OFF_G_RWKV = OFF_G_POOL + D_MODEL
IN_WIDTH = OFF_G_RWKV + D_MODEL
SHIFT_WIDTH = 3 * D_MODEL + 2 * D_LORA

MXU_TILE = 256
GROUP_HEADS = MXU_TILE // HEAD_SIZE
GROUP_W = MXU_TILE
N_GROUPS = D_MODEL // GROUP_W
CHUNK = 64
INVERSE_CHUNKS = 4
SEQ_TILE = 256
VMEM_LIMIT_BYTES = 56 * 1024 * 1024


def _sigmoid(z):
    return 1.0 / (1.0 + jnp.exp(-z))


def _silu(z):
    return z * _sigmoid(z)


def _dot(a, b):
    return jnp.dot(a, b, preferred_element_type=F32)


def _dot_nt(a, b):
    return lax.dot_general(a, b, (((1,), (1,)), ((), ())), preferred_element_type=F32)


def _dot_tn(a, b):
    return lax.dot_general(a, b, (((0,), (0,)), ((), ())), preferred_element_type=F32)


def _head_sums(z, blk_bf):
    outs = [_dot(z[:, ls].astype(BF16), blk_bf) for ls in _GROUP_LANES]
    return jnp.concatenate(outs, axis=1)


def _chunk_cumsum(z, row):
    t_in_chunk = row % CHUNK
    step = 1
    while step < CHUNK:
        z = z + jnp.where(t_in_chunk >= step, pltpu.roll(z, step, axis=0), 0.0)
        step *= 2
    n_chunks = z.shape[0] // CHUNK
    total = jnp.concatenate(
        [jnp.broadcast_to(z[(cc + 1) * CHUNK - 1:(cc + 1) * CHUNK, :], (CHUNK, z.shape[1])) for cc in range(n_chunks)],
        axis=0)
    return z, total


def _tile_rows(z):
    return jnp.concatenate([z] * GROUP_HEADS, axis=0)


def _tile_mask(z, blk_bf):
    return _tile_rows(z) * blk_bf


def _chunk_rows(c):
    return pl.ds(pl.multiple_of(c * CHUNK, CHUNK), CHUNK)


_GROUP_LANES = [slice(g * GROUP_W, (g + 1) * GROUP_W) for g in range(N_GROUPS)]


def _chunk_inverse_stage(c, refs, consts):
    at_s, rt_s, bt_s, kt_s, vbf_s, tcat_s, arb_s, fvw_s, fvy_s = refs
    blk_bf, m_s_cat, m_i_cat, eye = consts
    rows, lanes = [], []
    for cc in range(INVERSE_CHUNKS):
        for ls in _GROUP_LANES:
            rows.append(_chunk_rows(c * INVERSE_CHUNKS + cc))
            lanes.append(ls)
    groups = range(len(rows))
    lhs = [jnp.concatenate([at_s[rows[g], lanes[g]], rt_s[rows[g], lanes[g]]], axis=0) for g in groups]
    rhs = [jnp.concatenate([_tile_mask(bt_s[rows[g], lanes[g]], blk_bf),
                            _tile_mask(kt_s[rows[g], lanes[g]], blk_bf)], axis=0) for g in groups]
    a_all = [_dot_nt(lhs[g], rhs[g]) for g in groups]
    a_ab = [a_all[g][:CHUNK, :GROUP_W] * m_s_cat for g in groups]
    x_acc = [eye + a_ab[g] for g in groups]
    p_bf = [a_ab[g].astype(BF16) for g in groups]
    p_bf = [_dot(p_bf[g], _tile_mask(p_bf[g], blk_bf)).astype(BF16) for g in groups]
    n_levels = CHUNK.bit_length() - 1
    for level in range(1, n_levels):
        p_bd = [_tile_mask(p_bf[g], blk_bf) for g in groups]
        if level < n_levels - 1:
            both = [_dot(jnp.concatenate([x_acc[g].astype(BF16), p_bf[g]], axis=0), p_bd[g]) for g in groups]
            x_acc = [x_acc[g] + both[g][:CHUNK] for g in groups]
            p_bf = [both[g][CHUNK:].astype(BF16) for g in groups]
        else:
            x_acc = [x_acc[g] + _dot(x_acc[g].astype(BF16), p_bd[g]) for g in groups]
    for g in groups:
        tcat_s[rows[g], lanes[g]] = x_acc[g].astype(BF16)
        arb_s[rows[g], lanes[g]] = (a_all[g][CHUNK:, :GROUP_W] * m_i_cat).astype(BF16)
    from_v = [_dot(jnp.concatenate([a_all[g][:CHUNK, GROUP_W:] * m_s_cat, a_all[g][CHUNK:, GROUP_W:] * m_i_cat],
                                   axis=0).astype(BF16), _tile_mask(vbf_s[rows[g], lanes[g]], blk_bf))
              for g in groups]
    for g in groups:
        fvw_s[rows[g], lanes[g]] = from_v[g][:CHUNK]
        fvy_s[rows[g], lanes[g]] = from_v[g][CHUNK:]


def _chunk_state_stage(c, refs, consts):
    at_s, rt_s, vbf_s, tcat_s, arb_s, fvw_s, fvy_s, bend_s, kend_s, plast_s, y_s, state_ref = refs
    blk_bf, blk_f32 = consts
    rows = _chunk_rows(c)
    first_row = pl.ds(pl.multiple_of(c * CHUNK, CHUNK), 1)
    groups = range(N_GROUPS)
    lanes = _GROUP_LANES
    state = [state_ref[g] for g in groups]
    lhs = [jnp.concatenate([at_s[rows, ls], rt_s[rows, ls]], axis=0) for ls in lanes]
    from_state = [_dot_nt(lhs[g], state[g].astype(BF16)) for g in groups]
    w_bd = [_tile_mask((from_state[g][:CHUNK] + fvw_s[rows, lanes[g]]).astype(BF16), blk_bf) for g in groups]
    u_bf = [_dot(tcat_s[rows, lanes[g]], w_bd[g]).astype(BF16) for g in groups]
    y_u = [_dot(arb_s[rows, lanes[g]], _tile_mask(u_bf[g], blk_bf)) for g in groups]
    upd = [_dot_tn(jnp.concatenate([u_bf[g], vbf_s[rows, lanes[g]]], axis=0),
                   jnp.concatenate([bend_s[rows, lanes[g]], kend_s[rows, lanes[g]]], axis=0)) for g in groups]
    for g in groups:
        y_s[rows, lanes[g]] = from_state[g][CHUNK:] + fvy_s[rows, lanes[g]] + y_u[g]
        state_ref[g] = state[g] * plast_s[first_row, lanes[g]] + upd[g] * blk_f32


def _layer_kernel(*refs, first, last):
    it = iter(refs)
    x_ref = next(it)
    vfirst_ref = None if first else next(it)
    norm_g_ref = next(it)
    w_in_ref = next(it)
    pool_lin_ref = next(it)
    pool_scale_ref = next(it)
    w_pool_proj_ref = next(it)
    mu_ref = next(it)
    w0_ref = next(it)
    w2_ref = next(it)
    a0_ref = next(it)
    a2_ref = next(it)
    if not first:
        v0_ref = next(it)
        v1_ref = next(it)
        v2_ref = next(it)
    kk_scale_ref = next(it)
    ka_ref = next(it)
    rk_ref = next(it)
    lnx_w_ref = next(it)
    lnx_b_ref = next(it)
    w_rwkv_ref = next(it)
    w_out_ref = next(it)
    final_g_ref = next(it) if last else None
    blk_bf_ref = next(it)
    m_s_cat_ref = next(it)
    m_i_cat_ref = next(it)
    eye_ref = next(it)
    out_ref = next(it)
    vfirst_out_ref = next(it) if first else None
    state_ref = next(it)
    pool_halo_ref = next(it)
    shift_halo_ref = next(it)
    at_s, rt_s, bt_s, kt_s, bend_s, kend_s, vbf_s, tcat_s, arb_s = (next(it) for _ in range(9))
    plast_s, fvw_s, fvy_s, y_s, bonus_s = (next(it) for _ in range(5))

    ts = x_ref.shape[1]
    s_idx = pl.program_id(1)

    @pl.when(s_idx == 0)
    def _():
        state_ref[...] = jnp.zeros_like(state_ref)
        pool_halo_ref[...] = jnp.zeros_like(pool_halo_ref)
        shift_halo_ref[...] = jnp.zeros_like(shift_halo_ref)

    x = x_ref[0]
    h = (x * lax.rsqrt(jnp.mean(x * x, axis=-1, keepdims=True) + NORM_EPS) * norm_g_ref[...]).astype(BF16)

    def proj(off, width):
        return _dot(h, w_in_ref[:, off:off + width])

    row = lax.broadcasted_iota(jnp.int32, (ts, 1), 0)
    blk_bf = blk_bf_ref[...]

    u = proj(OFF_POOL_IN, D_MODEL)
    ext = jnp.concatenate([pool_halo_ref[...], u], axis=0)
    pool_halo_ref[...] = u[ts - POOL_HALO:, :]
    pos = (s_idx * ts + row + 1).astype(F32)
    mixed = []
    for gi, win in enumerate(POOL_WINDOWS):
        sl = slice(gi * POOL_GROUP, (gi + 1) * POOL_GROUP)
        acc = ext[:, sl]
        step = 1
        while step < win:
            acc = acc + pltpu.roll(acc, step, axis=0)
            step *= 2
        pooled = acc[POOL_HALO:, :] * (1.0 / jnp.minimum(pos, float(win)))
        mixed.append(_dot((pooled - u[:, sl]).astype(BF16), pool_lin_ref[gi]))
    mixed = jnp.concatenate(mixed, axis=1) * pool_scale_ref[...]
    y_pool = _dot((mixed * _silu(proj(OFF_POOL_GATE, D_MODEL))).astype(BF16), w_pool_proj_ref[...])

    def token_shift(z, off, width):
        prev = pltpu.roll(z, 1, axis=0)
        prev = jnp.where(row == 0, shift_halo_ref[0:1, off:off + width], prev)
        shift_halo_ref[0:1, off:off + width] = z[ts - 1:ts, :]
        return z + mu_ref[:, off:off + width] * (prev - z)

    r = token_shift(proj(OFF_R, D_MODEL), 0, D_MODEL)
    k = token_shift(proj(OFF_K, D_MODEL), D_MODEL, D_MODEL)
    v = token_shift(proj(OFF_V, D_MODEL), 2 * D_MODEL, D_MODEL)
    lora = token_shift(proj(OFF_LORA, 2 * D_LORA), 3 * D_MODEL, 2 * D_LORA)
    lw = -float(np.exp(-0.5)) * _sigmoid(w0_ref[...] + _dot(jnp.tanh(lora).astype(BF16), w2_ref[...]))
    a = _sigmoid(a0_ref[...] + _dot(lora.astype(BF16), a2_ref[...]))
    if first:
        vfirst_out_ref[0] = v
    else:
        low = _dot(v.astype(BF16), v1_ref[...])
        mix = _sigmoid(v0_ref[...] + _dot(low.astype(BF16), v2_ref[...]))
        v = v + (vfirst_ref[0] - v) * mix
    kk = k * kk_scale_ref[...]
    kk = kk * lax.rsqrt(jnp.maximum(_head_sums(kk * kk, blk_bf), 1e-24))
    k = k * (1.0 + (a - 1.0) * ka_ref[...])
    bonus_s[...] = _head_sums(r * k * rk_ref[...], blk_bf) * v

    cum, total = _chunk_cumsum(lw, row)
    e_neg = jnp.exp(-cum)
    e_tail = jnp.exp(total - cum)
    kka = kk * a
    at_s[...] = (-kk * jnp.exp(cum - lw)).astype(BF16)
    rt_s[...] = (r * jnp.exp(cum)).astype(BF16)
    bt_s[...] = (kka * e_neg).astype(BF16)
    kt_s[...] = (k * e_neg).astype(BF16)
    bend_s[...] = (kka * e_tail).astype(BF16)
    kend_s[...] = (k * e_tail).astype(BF16)
    vbf_s[...] = v.astype(BF16)
    plast_s[...] = jnp.exp(total)

    inv_refs = (at_s, rt_s, bt_s, kt_s, vbf_s, tcat_s, arb_s, fvw_s, fvy_s)
    inv_consts = (blk_bf, m_s_cat_ref[...], m_i_cat_ref[...], eye_ref[...])

    def inverse_body(c, carry):
        _chunk_inverse_stage(c, inv_refs, inv_consts)
        return carry

    lax.fori_loop(0, ts // (CHUNK * INVERSE_CHUNKS), inverse_body, 0)

    st_refs = (at_s, rt_s, vbf_s, tcat_s, arb_s, fvw_s, fvy_s, bend_s, kend_s, plast_s, y_s, state_ref)
    st_consts = (blk_bf, blk_bf.astype(F32))

    def state_body(c, carry):
        _chunk_state_stage(c, st_refs, st_consts)
        return carry

    lax.fori_loop(0, ts // CHUNK, state_body, 0)

    y = y_s[...]
    inv_n = 1.0 / HEAD_SIZE
    dev = y - _head_sums(y, blk_bf) * inv_n
    var = _head_sums(dev * dev, blk_bf) * inv_n
    y = dev * lax.rsqrt(var + LNX_EPS) * lnx_w_ref[...] + lnx_b_ref[...] + bonus_s[...]
    y_rwkv = _dot((y * _silu(proj(OFF_RWKV_GATE, D_MODEL))).astype(BF16), w_rwkv_ref[...])

    merged = (_sigmoid(proj(OFF_G_POOL, D_MODEL)) * y_pool
              + _sigmoid(proj(OFF_G_RWKV, D_MODEL)) * y_rwkv)
    xo = x + _dot(merged.astype(BF16), w_out_ref[...])
    if last:
        xo = xo * lax.rsqrt(jnp.mean(xo * xo, axis=-1, keepdims=True) + NORM_EPS) * final_g_ref[...]
    out_ref[0] = xo


def _scan_constants():
    idx = np.arange(GROUP_W)
    same_head = (idx[:, None] // HEAD_SIZE) == (idx[None, :] // HEAD_SIZE)
    t_col = idx[None, :] % CHUNK
    t64 = np.arange(CHUNK)[:, None]
    blk = same_head.astype(np.float32)
    m_s_cat = (t_col < t64).astype(np.float32)
    m_i_cat = (t_col <= t64).astype(np.float32)
    eye_cat = (t_col == t64).astype(np.float32)
    return (jnp.asarray(blk, BF16), jnp.asarray(m_s_cat), jnp.asarray(m_i_cat), jnp.asarray(eye_cat))


def _resident(arr):
    nd = arr.ndim
    return pl.BlockSpec(arr.shape, lambda b, s, _nd=nd: (0,) * _nd, pipeline_mode=pl.Buffered(1))


def _layer_call(x, vfirst, weights, consts, *, first, last):
    batch, seq, _ = x.shape
    ts = SEQ_TILE
    assert seq % ts == 0 and ts % CHUNK == 0 and ts >= POOL_HALO
    tile = pl.BlockSpec((1, ts, D_MODEL), lambda b, s: (b, s, 0))
    operands = [x] + ([] if first else [vfirst]) + list(weights) + list(consts)
    in_specs = [tile] + ([] if first else [tile]) + [_resident(w) for w in list(weights) + list(consts)]
    act = jax.ShapeDtypeStruct(x.shape, F32)
    out_shape = (act, act) if first else act
    out_specs = (tile, tile) if first else tile
    scratch = [
        pltpu.VMEM((N_GROUPS, GROUP_W, GROUP_W), F32),
        pltpu.VMEM((POOL_HALO, D_MODEL), F32),
        pltpu.VMEM((8, SHIFT_WIDTH), F32),
    ]
    scratch += [pltpu.VMEM((ts, D_MODEL), BF16) for _ in range(9)]
    scratch += [pltpu.VMEM((ts, D_MODEL), F32) for _ in range(5)]
    return pl.pallas_call(
        functools.partial(_layer_kernel, first=first, last=last),
        grid=(batch, seq // ts),
        in_specs=in_specs,
        out_specs=out_specs,
        out_shape=out_shape,
        scratch_shapes=scratch,
        compiler_params=pltpu.CompilerParams(
            dimension_semantics=("arbitrary", "arbitrary"),
            vmem_limit_bytes=VMEM_LIMIT_BYTES),
        name="rwkv_pool_layer",
    )(*operands)


def _pad_rows(w, rows, offset):
    out = jnp.zeros((rows, w.shape[1]), w.dtype)
    return out.at[offset:offset + w.shape[0]].set(w)


def kernel(x, norm_g, w_in, pool_lin, pool_scale, w_pool_proj, mu_shift, decay_w0, decay_w2, a0, a2,
           vres_v0, vres_v1, vres_v2, k_k, k_a, r_k, lnx_w, lnx_b, w_rwkv_proj, w_out, final_g):
    consts = _scan_constants()
    row = lambda p: p.reshape(1, -1).astype(F32)
    vfirst = None
    for i in range(DEPTH):
        first, last = i == 0, i == DEPTH - 1
        weights = [row(norm_g[i]), w_in[i].astype(BF16), pool_lin[i].astype(BF16), row(pool_scale[i]),
                   w_pool_proj[i].astype(BF16), row(mu_shift[i]), row(decay_w0[i]),
                   _pad_rows(decay_w2[i], 2 * D_LORA, 0).astype(BF16), row(a0[i]),
                   _pad_rows(a2[i], 2 * D_LORA, D_LORA).astype(BF16)]
        if not first:
            v1 = jnp.zeros((D_MODEL, LANE), F32).at[:, :D_MV_LORA].set(vres_v1[i - 1])
            weights += [row(vres_v0[i - 1]), v1.astype(BF16),
                        _pad_rows(vres_v2[i - 1], LANE, 0).astype(BF16)]
        weights += [row(k_k[i]), row(k_a[i]), row(r_k[i]), row(lnx_w[i]), row(lnx_b[i]),
                    w_rwkv_proj[i].astype(BF16), w_out[i].astype(BF16)]
        if last:
            weights.append(row(final_g))
        res = _layer_call(x, vfirst, weights, consts, first=first, last=last)
        if first:
            x, vfirst = res
        else:
            x = res
    return x
```

```python
import functools

import numpy as np
import jax
import jax.numpy as jnp
from jax import lax
from jax.experimental import pallas as pl
from jax.experimental.pallas import tpu as pltpu

F32 = jnp.float32
BF16 = jnp.bfloat16

D_MODEL = 1024
DEPTH = 4
POOL_WINDOWS = (2, 4, 8, 16)
POOL_GROUP = D_MODEL // len(POOL_WINDOWS)
POOL_HALO = max(POOL_WINDOWS)
HEAD_SIZE = 64
D_LORA = 64
D_MV_LORA = 32
LANE = 128
NORM_EPS = 1e-6
LNX_EPS = 1e-5 * HEAD_SIZE

OFF_POOL_IN = 0
OFF_POOL_GATE = D_MODEL
OFF_R = 2 * D_MODEL
OFF_K = 3 * D_MODEL
OFF_V = 4 * D_MODEL
OFF_LORA = 5 * D_MODEL
OFF_RWKV_GATE = OFF_LORA + 2 * D_LORA
OFF_G_POOL = OFF_RWKV_GATE + D_MODEL
OFF_G_RWKV = OFF_G_POOL + D_MODEL
IN_WIDTH = OFF_G_RWKV + D_MODEL
SHIFT_WIDTH = 3 * D_MODEL + 2 * D_LORA

MXU_TILE = 256
GROUP_HEADS = MXU_TILE // HEAD_SIZE
GROUP_W = MXU_TILE
N_GROUPS = D_MODEL // GROUP_W
CHUNK = 64
SEQ_TILE = 256
VMEM_LIMIT_BYTES = 56 * 1024 * 1024


def _sigmoid(z):
    return 0.5 * jnp.tanh(0.5 * z) + 0.5


def _silu(z):
    return z * _sigmoid(z)


def _dot(a, b):
    return jnp.dot(a, b, preferred_element_type=F32)


def _dot_nt(a, b):
    return lax.dot_general(a, b, (((1,), (1,)), ((), ())), preferred_element_type=F32)


def _dot_tn(a, b):
    return lax.dot_general(a, b, (((0,), (0,)), ((), ())), preferred_element_type=F32)


def _head_sums(z, blk_bf):
    outs = [_dot(z[:, ls].astype(BF16), blk_bf) for ls in _GROUP_LANES]
    return jnp.concatenate(outs, axis=1)


def _chunk_cumsum(z, row):
    t_in_chunk = row % CHUNK
    step = 1
    while step < CHUNK:
        z = z + jnp.where(t_in_chunk >= step, pltpu.roll(z, step, axis=0), 0.0)
        step *= 2
    n_chunks = z.shape[0] // CHUNK
    total = jnp.concatenate(
        [jnp.broadcast_to(z[(cc + 1) * CHUNK - 1:(cc + 1) * CHUNK, :], (CHUNK, z.shape[1])) for cc in range(n_chunks)],
        axis=0)
    return z, total


def _tile_rows(z):
    return jnp.concatenate([z] * GROUP_HEADS, axis=0)


def _tile_mask(z, blk_bf):
    return _tile_rows(z) * blk_bf


_GROUP_LANES = [slice(g * GROUP_W, (g + 1) * GROUP_W) for g in range(N_GROUPS)]


def _interleave(chains):
    active = list(chains)
    while active:
        still = []
        for chain in active:
            try:
                next(chain)
                still.append(chain)
            except StopIteration:
                pass
        active = still


def _inverse_chain(rows, ls, refs, consts):
    at_s, rt_s, bt_s, kt_s, vbf_s, tcat_s, arb_s, fvw_s, fvy_s = refs
    blk_bf, m_s_cat, m_i_cat, eye = consts
    lhs = jnp.concatenate([at_s[rows, ls], rt_s[rows, ls]], axis=0)
    rhs = jnp.concatenate([_tile_mask(bt_s[rows, ls], blk_bf), _tile_mask(kt_s[rows, ls], blk_bf)], axis=0)
    a_all = _dot_nt(lhs, rhs)
    yield
    a_ab = a_all[:CHUNK, :GROUP_W] * m_s_cat
    x_acc = eye + a_ab
    p_bf = a_ab.astype(BF16)
    p_bf = _dot(p_bf, _tile_mask(p_bf, blk_bf)).astype(BF16)
    yield
    n_levels = CHUNK.bit_length() - 1
    for level in range(1, n_levels):
        p_bd = _tile_mask(p_bf, blk_bf)
        if level < n_levels - 1:
            both = _dot(jnp.concatenate([x_acc.astype(BF16), p_bf], axis=0), p_bd)
            x_acc = x_acc + both[:CHUNK]
            p_bf = both[CHUNK:].astype(BF16)
        else:
            x_acc = x_acc + _dot(x_acc.astype(BF16), p_bd)
        yield
    tcat_s[rows, ls] = x_acc.astype(BF16)
    arb_s[rows, ls] = (a_all[CHUNK:, :GROUP_W] * m_i_cat).astype(BF16)
    from_v = _dot(jnp.concatenate([a_all[:CHUNK, GROUP_W:] * m_s_cat, a_all[CHUNK:, GROUP_W:] * m_i_cat],
                                  axis=0).astype(BF16), _tile_mask(vbf_s[rows, ls], blk_bf))
    yield
    fvw_s[rows, ls] = from_v[:CHUNK]
    fvy_s[rows, ls] = from_v[CHUNK:]


def _state_chain(rows, g, refs, consts):
    at_s, rt_s, vbf_s, tcat_s, arb_s, fvw_s, fvy_s, bend_s, kend_s, plast_s, y_s, state_ref = refs
    blk_bf, blk_f32 = consts
    ls = _GROUP_LANES[g]
    state = state_ref[g]
    lhs = jnp.concatenate([at_s[rows, ls], rt_s[rows, ls]], axis=0)
    from_state = _dot_nt(lhs, state.astype(BF16))
    yield
    yield
    w_bd = _tile_mask((from_state[:CHUNK] + fvw_s[rows, ls]).astype(BF16), blk_bf)
    u_bf = _dot(tcat_s[rows, ls], w_bd).astype(BF16)
    yield
    yield
    y_u = _dot(arb_s[rows, ls], _tile_mask(u_bf, blk_bf))
    upd = _dot_tn(jnp.concatenate([u_bf, vbf_s[rows, ls]], axis=0),
                  jnp.concatenate([bend_s[rows, ls], kend_s[rows, ls]], axis=0))
    yield
    yield
    y_s[rows, ls] = from_state[CHUNK:] + fvy_s[rows, ls] + y_u
    state_ref[g] = state * plast_s[rows.start:rows.start + 1, ls] + upd * blk_f32


def _layer_kernel(*refs, first, last):
    it = iter(refs)
    x_ref = next(it)
    vfirst_ref = None if first else next(it)
    norm_g_ref = next(it)
    w_in_ref = next(it)
    pool_lin_ref = next(it)
    pool_scale_ref = next(it)
    w_pool_proj_ref = next(it)
    mu_ref = next(it)
    w0_ref = next(it)
    w2_ref = next(it)
    a0_ref = next(it)
    a2_ref = next(it)
    if not first:
        v0_ref = next(it)
        v1_ref = next(it)
        v2_ref = next(it)
    kk_scale_ref = next(it)
    ka_ref = next(it)
    rk_ref = next(it)
    lnx_w_ref = next(it)
    lnx_b_ref = next(it)
    w_rwkv_ref = next(it)
    w_out_ref = next(it)
    final_g_ref = next(it) if last else None
    blk_bf_ref = next(it)
    m_s_cat_ref = next(it)
    m_i_cat_ref = next(it)
    eye_ref = next(it)
    out_ref = next(it)
    vfirst_out_ref = next(it) if first else None
    state_ref = next(it)
    pool_halo_ref = next(it)
    shift_halo_ref = next(it)
    at_s, rt_s, bt_s, kt_s, bend_s, kend_s, vbf_s, tcat_s, arb_s = (next(it) for _ in range(9))
    plast_s, fvw_s, fvy_s, y_s, bonus_s = (next(it) for _ in range(5))

    ts = x_ref.shape[1]
    s_idx = pl.program_id(1)

    @pl.when(s_idx == 0)
    def _():
        state_ref[...] = jnp.zeros_like(state_ref)
        pool_halo_ref[...] = jnp.zeros_like(pool_halo_ref)
        shift_halo_ref[...] = jnp.zeros_like(shift_halo_ref)

    x = x_ref[0]
    h = (x * lax.rsqrt(jnp.mean(x * x, axis=-1, keepdims=True) + NORM_EPS) * norm_g_ref[...]).astype(BF16)

    def proj(off, width):
        return _dot(h, w_in_ref[:, off:off + width])

    row = lax.broadcasted_iota(jnp.int32, (ts, 1), 0)
    blk_bf = blk_bf_ref[...]

    def token_shift(z, off, width):
        prev = pltpu.roll(z, 1, axis=0)
        prev = jnp.where(row == 0, shift_halo_ref[0:1, off:off + width], prev)
        shift_halo_ref[0:1, off:off + width] = z[ts - 1:ts, :]
        return z + mu_ref[:, off:off + width] * (prev - z)

    z_lora = proj(OFF_LORA, 2 * D_LORA)
    z_k = proj(OFF_K, D_MODEL)
    lora = token_shift(z_lora, 3 * D_MODEL, 2 * D_LORA)
    lw = -float(np.exp(-0.5)) * _sigmoid(w0_ref[...] + _dot(jnp.tanh(lora).astype(BF16), w2_ref[...]))
    a = _sigmoid(a0_ref[...] + _dot(lora.astype(BF16), a2_ref[...]))
    z_r = proj(OFF_R, D_MODEL)
    k = token_shift(z_k, D_MODEL, D_MODEL)
    kk = k * kk_scale_ref[...]
    kk = kk * lax.rsqrt(jnp.maximum(_head_sums(kk * kk, blk_bf), 1e-24))
    k = k * (1.0 + (a - 1.0) * ka_ref[...])
    z_v = proj(OFF_V, D_MODEL)
    cum, total = _chunk_cumsum(lw, row)
    e_neg = jnp.exp(-cum)
    e_tail = jnp.exp(total - cum)
    kka = kk * a
    at_s[...] = (-kk * jnp.exp(cum - lw)).astype(BF16)
    bt_s[...] = (kka * e_neg).astype(BF16)
    kt_s[...] = (k * e_neg).astype(BF16)
    bend_s[...] = (kka * e_tail).astype(BF16)
    kend_s[...] = (k * e_tail).astype(BF16)
    plast_s[...] = jnp.exp(total)
    u = proj(OFF_POOL_IN, D_MODEL)
    r = token_shift(z_r, 0, D_MODEL)
    rt_s[...] = (r * jnp.exp(cum)).astype(BF16)
    rk_sum = _head_sums(r * k * rk_ref[...], blk_bf)
    z_pool_gate = proj(OFF_POOL_GATE, D_MODEL)
    v = token_shift(z_v, 2 * D_MODEL, D_MODEL)
    if first:
        vfirst_out_ref[0] = v
    else:
        low = _dot(v.astype(BF16), v1_ref[...])
        mix = _sigmoid(v0_ref[...] + _dot(low.astype(BF16), v2_ref[...]))
        v = v + (vfirst_ref[0] - v) * mix
    vbf_s[...] = v.astype(BF16)
    bonus_s[...] = rk_sum * v

    ext = jnp.concatenate([pool_halo_ref[...], u], axis=0)
    pool_halo_ref[...] = u[ts - POOL_HALO:, :]
    pos = (s_idx * ts + row + 1).astype(F32)
    mixed = []
    for gi, win in enumerate(POOL_WINDOWS):
        sl = slice(gi * POOL_GROUP, (gi + 1) * POOL_GROUP)
        acc = ext[:, sl]
        step = 1
        while step < win:
            acc = acc + pltpu.roll(acc, step, axis=0)
            step *= 2
        pooled = acc[POOL_HALO:, :] * (1.0 / jnp.minimum(pos, float(win)))
        mixed.append(_dot((pooled - u[:, sl]).astype(BF16), pool_lin_ref[gi]))
    mixed = jnp.concatenate(mixed, axis=1) * pool_scale_ref[...]
    y_pool = _dot((mixed * _silu(z_pool_gate)).astype(BF16), w_pool_proj_ref[...])

    inv_refs = (at_s, rt_s, bt_s, kt_s, vbf_s, tcat_s, arb_s, fvw_s, fvy_s)
    inv_consts = (blk_bf, m_s_cat_ref[...], m_i_cat_ref[...], eye_ref[...])
    st_refs = (at_s, rt_s, vbf_s, tcat_s, arb_s, fvw_s, fvy_s, bend_s, kend_s, plast_s, y_s, state_ref)
    st_consts = (blk_bf, blk_bf.astype(F32))
    n_chunks = ts // CHUNK
    chunk_rows = [slice(c * CHUNK, (c + 1) * CHUNK) for c in range(n_chunks)]
    for step in range(n_chunks + 1):
        chains = []
        if step >= 1:
            chains += [_state_chain(chunk_rows[step - 1], g, st_refs, st_consts) for g in range(N_GROUPS)]
        if step < n_chunks:
            chains += [_inverse_chain(chunk_rows[step], ls, inv_refs, inv_consts) for ls in _GROUP_LANES]
        _interleave(chains)

    y = y_s[...]
    inv_n = 1.0 / HEAD_SIZE
    dev = y - _head_sums(y, blk_bf) * inv_n
    var = _head_sums(dev * dev, blk_bf) * inv_n
    y = dev * lax.rsqrt(var + LNX_EPS) * lnx_w_ref[...] + lnx_b_ref[...] + bonus_s[...]
    y_rwkv = _dot((y * _silu(proj(OFF_RWKV_GATE, D_MODEL))).astype(BF16), w_rwkv_ref[...])

    merged = (_sigmoid(proj(OFF_G_POOL, D_MODEL)) * y_pool
              + _sigmoid(proj(OFF_G_RWKV, D_MODEL)) * y_rwkv)
    xo = x + _dot(merged.astype(BF16), w_out_ref[...])
    if last:
        xo = xo * lax.rsqrt(jnp.mean(xo * xo, axis=-1, keepdims=True) + NORM_EPS) * final_g_ref[...]
    out_ref[0] = xo


def _scan_constants():
    idx = np.arange(GROUP_W)
    same_head = (idx[:, None] // HEAD_SIZE) == (idx[None, :] // HEAD_SIZE)
    t_col = idx[None, :] % CHUNK
    t64 = np.arange(CHUNK)[:, None]
    blk = same_head.astype(np.float32)
    m_s_cat = (t_col < t64).astype(np.float32)
    m_i_cat = (t_col <= t64).astype(np.float32)
    eye_cat = (t_col == t64).astype(np.float32)
    return (jnp.asarray(blk, BF16), jnp.asarray(m_s_cat), jnp.asarray(m_i_cat), jnp.asarray(eye_cat))


def _resident(arr):
    nd = arr.ndim
    return pl.BlockSpec(arr.shape, lambda b, s, _nd=nd: (0,) * _nd, pipeline_mode=pl.Buffered(1))


def _layer_call(x, vfirst, weights, consts, *, first, last):
    batch, seq, _ = x.shape
    ts = SEQ_TILE
    assert seq % ts == 0 and ts % CHUNK == 0 and ts >= POOL_HALO
    tile = pl.BlockSpec((1, ts, D_MODEL), lambda b, s: (b, s, 0))
    operands = [x] + ([] if first else [vfirst]) + list(weights) + list(consts)
    in_specs = [tile] + ([] if first else [tile]) + [_resident(w) for w in list(weights) + list(consts)]
    act = jax.ShapeDtypeStruct(x.shape, F32)
    out_shape = (act, act) if first else act
    out_specs = (tile, tile) if first else tile
    scratch = [
        pltpu.VMEM((N_GROUPS, GROUP_W, GROUP_W), F32),
        pltpu.VMEM((POOL_HALO, D_MODEL), F32),
        pltpu.VMEM((8, SHIFT_WIDTH), F32),
    ]
    scratch += [pltpu.VMEM((ts, D_MODEL), BF16) for _ in range(9)]
    scratch += [pltpu.VMEM((ts, D_MODEL), F32) for _ in range(5)]
    return pl.pallas_call(
        functools.partial(_layer_kernel, first=first, last=last),
        grid=(batch, seq // ts),
        in_specs=in_specs,
        out_specs=out_specs,
        out_shape=out_shape,
        scratch_shapes=scratch,
        compiler_params=pltpu.CompilerParams(
            dimension_semantics=("arbitrary", "arbitrary"),
            vmem_limit_bytes=VMEM_LIMIT_BYTES),
        name="rwkv_pool_layer",
    )(*operands)


def _pad_rows(w, rows, offset):
    out = jnp.zeros((rows, w.shape[1]), w.dtype)
    return out.at[offset:offset + w.shape[0]].set(w)


def kernel(x, norm_g, w_in, pool_lin, pool_scale, w_pool_proj, mu_shift, decay_w0, decay_w2, a0, a2,
           vres_v0, vres_v1, vres_v2, k_k, k_a, r_k, lnx_w, lnx_b, w_rwkv_proj, w_out, final_g):
    consts = _scan_constants()
    row = lambda p: p.reshape(1, -1).astype(F32)
    vfirst = None
    for i in range(DEPTH):
        first, last = i == 0, i == DEPTH - 1
        weights = [row(norm_g[i]), w_in[i].astype(BF16), pool_lin[i].astype(BF16), row(pool_scale[i]),
                   w_pool_proj[i].astype(BF16), row(mu_shift[i]), row(decay_w0[i]),
                   _pad_rows(decay_w2[i], 2 * D_LORA, 0).astype(BF16), row(a0[i]),
                   _pad_rows(a2[i], 2 * D_LORA, D_LORA).astype(BF16)]
        if not first:
            v1 = jnp.zeros((D_MODEL, LANE), F32).at[:, :D_MV_LORA].set(vres_v1[i - 1])
            weights += [row(vres_v0[i - 1]), v1.astype(BF16),
                        _pad_rows(vres_v2[i - 1], LANE, 0).astype(BF16)]
        weights += [row(k_k[i]), row(k_a[i]), row(r_k[i]), row(lnx_w[i]), row(lnx_b[i]),
                    w_rwkv_proj[i].astype(BF16), w_out[i].astype(BF16)]
        if last:
            weights.append(row(final_g))
        res = _layer_call(x, vfirst, weights, consts, first=first, last=last)
        if first:
            x, vfirst = res
        else:
            x = res
    return x
```

```python
import functools

import numpy as np
import jax
import jax.numpy as jnp
from jax import lax
from jax.experimental import pallas as pl
from jax.experimental.pallas import tpu as pltpu

F32 = jnp.float32
BF16 = jnp.bfloat16

D_MODEL = 1024
DEPTH = 4
POOL_WINDOWS = (2, 4, 8, 16)
POOL_GROUP = D_MODEL // len(POOL_WINDOWS)
POOL_HALO = max(POOL_WINDOWS)
HEAD_SIZE = 64
D_LORA = 64
D_MV_LORA = 32
LANE = 128
SUBLANE = 8
NORM_EPS = 1e-6
LNX_EPS = 1e-5 * HEAD_SIZE

OFF_POOL_IN = 0
OFF_POOL_GATE = D_MODEL
OFF_R = 2 * D_MODEL
OFF_K = 3 * D_MODEL
OFF_V = 4 * D_MODEL
OFF_LORA = 5 * D_MODEL
OFF_RWKV_GATE = OFF_LORA + 2 * D_LORA
OFF_G_POOL = OFF_RWKV_GATE + D_MODEL
OFF_G_RWKV = OFF_G_POOL + D_MODEL
IN_WIDTH = OFF_G_RWKV + D_MODEL
SHIFT_WIDTH = 3 * D_MODEL + 2 * D_LORA

MXU_TILE = 256
GROUP_HEADS = MXU_TILE // HEAD_SIZE
GROUP_W = MXU_TILE
N_GROUPS = D_MODEL // GROUP_W
CHUNK = 64
SEQ_TILE = 128
SEQS_PER_TILE = 2
VMEM_LIMIT_BYTES = 56 * 1024 * 1024


def _sigmoid(z):
    return 0.5 * jnp.tanh(0.5 * z) + 0.5


def _silu(z):
    return z * _sigmoid(z)


def _dot(a, b):
    return jnp.dot(a, b, preferred_element_type=F32)


def _dot_nt(a, b):
    return lax.dot_general(a, b, (((1,), (1,)), ((), ())), preferred_element_type=F32)


def _dot_tn(a, b):
    return lax.dot_general(a, b, (((0,), (0,)), ((), ())), preferred_element_type=F32)


def _head_sums(z, blk_bf):
    outs = [_dot(z[:, ls].astype(BF16), blk_bf) for ls in _GROUP_LANES]
    return jnp.concatenate(outs, axis=1)


def _chunk_cumsum(z, row):
    t_in_chunk = row % CHUNK
    step = 1
    while step < CHUNK:
        z = z + jnp.where(t_in_chunk >= step, pltpu.roll(z, step, axis=0), 0.0)
        step *= 2
    n_chunks = z.shape[0] // CHUNK
    total = jnp.concatenate(
        [jnp.broadcast_to(z[(cc + 1) * CHUNK - 1:(cc + 1) * CHUNK, :], (CHUNK, z.shape[1])) for cc in range(n_chunks)],
        axis=0)
    return z, total


def _tile_rows(z):
    return jnp.concatenate([z] * GROUP_HEADS, axis=0)


def _tile_mask(z, blk_bf):
    return _tile_rows(z) * blk_bf


_GROUP_LANES = [slice(g * GROUP_W, (g + 1) * GROUP_W) for g in range(N_GROUPS)]


def _interleave(chains):
    active = list(chains)
    while active:
        still = []
        for chain in active:
            try:
                next(chain)
                still.append(chain)
            except StopIteration:
                pass
        active = still


def _inverse_chain(rows, ls, refs, consts):
    at_s, rt_s, bt_s, kt_s, vbf_s, tcat_s, arb_s, fvw_s, fvy_s = refs
    blk_bf, m_s_cat, m_i_cat, eye = consts
    lhs = jnp.concatenate([at_s[rows, ls], rt_s[rows, ls]], axis=0)
    rhs = jnp.concatenate([_tile_mask(bt_s[rows, ls], blk_bf), _tile_mask(kt_s[rows, ls], blk_bf)], axis=0)
    a_all = _dot_nt(lhs, rhs)
    yield
    a_ab = a_all[:CHUNK, :GROUP_W] * m_s_cat
    x_acc = eye + a_ab
    p_bf = a_ab.astype(BF16)
    p_bf = _dot(p_bf, _tile_mask(p_bf, blk_bf)).astype(BF16)
    yield
    n_levels = CHUNK.bit_length() - 1
    for level in range(1, n_levels):
        p_bd = _tile_mask(p_bf, blk_bf)
        if level < n_levels - 1:
            both = _dot(jnp.concatenate([x_acc.astype(BF16), p_bf], axis=0), p_bd)
            x_acc = x_acc + both[:CHUNK]
            p_bf = both[CHUNK:].astype(BF16)
        else:
            x_acc = x_acc + _dot(x_acc.astype(BF16), p_bd)
        yield
    tcat_s[rows, ls] = x_acc.astype(BF16)
    arb_s[rows, ls] = (a_all[CHUNK:, :GROUP_W] * m_i_cat).astype(BF16)
    from_v = _dot(jnp.concatenate([a_all[:CHUNK, GROUP_W:] * m_s_cat, a_all[CHUNK:, GROUP_W:] * m_i_cat],
                                  axis=0).astype(BF16), _tile_mask(vbf_s[rows, ls], blk_bf))
    yield
    fvw_s[rows, ls] = from_v[:CHUNK]
    fvy_s[rows, ls] = from_v[CHUNK:]


def _state_chain(rows, state_idx, g, refs, consts):
    at_s, rt_s, vbf_s, tcat_s, arb_s, fvw_s, fvy_s, bend_s, kend_s, plast_s, y_s, state_ref = refs
    blk_bf, blk_f32 = consts
    ls = _GROUP_LANES[g]
    state = state_ref[state_idx]
    lhs = jnp.concatenate([at_s[rows, ls], rt_s[rows, ls]], axis=0)
    from_state = _dot_nt(lhs, state.astype(BF16))
    yield
    yield
    w_bd = _tile_mask((from_state[:CHUNK] + fvw_s[rows, ls]).astype(BF16), blk_bf)
    u_bf = _dot(tcat_s[rows, ls], w_bd).astype(BF16)
    yield
    yield
    y_u = _dot(arb_s[rows, ls], _tile_mask(u_bf, blk_bf))
    upd = _dot_tn(jnp.concatenate([u_bf, vbf_s[rows, ls]], axis=0),
                  jnp.concatenate([bend_s[rows, ls], kend_s[rows, ls]], axis=0))
    yield
    yield
    y_s[rows, ls] = from_state[CHUNK:] + fvy_s[rows, ls] + y_u
    state_ref[state_idx] = state * plast_s[rows.start:rows.start + 1, ls] + upd * blk_f32


def _layer_kernel(*refs, first, last):
    it = iter(refs)
    x_ref = next(it)
    vfirst_ref = None if first else next(it)
    norm_g_ref = next(it)
    w_in_ref = next(it)
    pool_lin_ref = next(it)
    pool_scale_ref = next(it)
    w_pool_proj_ref = next(it)
    mu_ref = next(it)
    w0_ref = next(it)
    w2_ref = next(it)
    a0_ref = next(it)
    a2_ref = next(it)
    if not first:
        v0_ref = next(it)
        v1_ref = next(it)
        v2_ref = next(it)
    kk_scale_ref = next(it)
    ka_ref = next(it)
    rk_ref = next(it)
    lnx_w_ref = next(it)
    lnx_b_ref = next(it)
    w_rwkv_ref = next(it)
    w_out_ref = next(it)
    final_g_ref = next(it) if last else None
    blk_bf_ref = next(it)
    m_s_cat_ref = next(it)
    m_i_cat_ref = next(it)
    eye_ref = next(it)
    out_ref = next(it)
    vfirst_out_ref = next(it) if first else None
    state_ref = next(it)
    pool_halo_ref = next(it)
    shift_halo_ref = next(it)
    at_s, rt_s, bt_s, kt_s, bend_s, kend_s, vbf_s, tcat_s, arb_s = (next(it) for _ in range(9))
    plast_s, fvw_s, fvy_s, y_s, bonus_s = (next(it) for _ in range(5))

    n_seqs, ts = x_ref.shape[0], x_ref.shape[1]
    n_rows = n_seqs * ts
    s_idx = pl.program_id(1)

    @pl.when(s_idx == 0)
    def _():
        state_ref[...] = jnp.zeros_like(state_ref)
        pool_halo_ref[...] = jnp.zeros_like(pool_halo_ref)
        shift_halo_ref[...] = jnp.zeros_like(shift_halo_ref)

    x = x_ref[...].reshape(n_rows, D_MODEL)
    h = (x * lax.rsqrt(jnp.mean(x * x, axis=-1, keepdims=True) + NORM_EPS) * norm_g_ref[...]).astype(BF16)

    def proj(off, width):
        return _dot(h, w_in_ref[:, off:off + width])

    row = lax.broadcasted_iota(jnp.int32, (n_rows, 1), 0)
    tok = row % ts
    blk_bf = blk_bf_ref[...]

    def token_shift(z, off, width):
        prev = pltpu.roll(z, 1, axis=0)
        for q in range(n_seqs):
            halo_row = slice(q * SUBLANE, q * SUBLANE + 1)
            prev = jnp.where(row == q * ts, shift_halo_ref[halo_row, off:off + width], prev)
            shift_halo_ref[halo_row, off:off + width] = z[(q + 1) * ts - 1:(q + 1) * ts, :]
        return z + mu_ref[:, off:off + width] * (prev - z)

    z_lora = proj(OFF_LORA, 2 * D_LORA)
    z_k = proj(OFF_K, D_MODEL)
    lora = token_shift(z_lora, 3 * D_MODEL, 2 * D_LORA)
    lw = -float(np.exp(-0.5)) * _sigmoid(w0_ref[...] + _dot(jnp.tanh(lora).astype(BF16), w2_ref[...]))
    a = _sigmoid(a0_ref[...] + _dot(lora.astype(BF16), a2_ref[...]))
    z_r = proj(OFF_R, D_MODEL)
    k = token_shift(z_k, D_MODEL, D_MODEL)
    kk = k * kk_scale_ref[...]
    kk = kk * lax.rsqrt(jnp.maximum(_head_sums(kk * kk, blk_bf), 1e-24))
    k = k * (1.0 + (a - 1.0) * ka_ref[...])
    z_v = proj(OFF_V, D_MODEL)
    cum, total = _chunk_cumsum(lw, row)
    e_neg = jnp.exp(-cum)
    e_tail = jnp.exp(total - cum)
    kka = kk * a
    at_s[...] = (-kk * jnp.exp(cum - lw)).astype(BF16)
    bt_s[...] = (kka * e_neg).astype(BF16)
    kt_s[...] = (k * e_neg).astype(BF16)
    bend_s[...] = (kka * e_tail).astype(BF16)
    kend_s[...] = (k * e_tail).astype(BF16)
    plast_s[...] = jnp.exp(total)
    u = proj(OFF_POOL_IN, D_MODEL)
    r = token_shift(z_r, 0, D_MODEL)
    rt_s[...] = (r * jnp.exp(cum)).astype(BF16)
    rk_sum = _head_sums(r * k * rk_ref[...], blk_bf)
    z_pool_gate = proj(OFF_POOL_GATE, D_MODEL)
    v = token_shift(z_v, 2 * D_MODEL, D_MODEL)
    if first:
        vfirst_out_ref[...] = v.reshape(n_seqs, ts, D_MODEL)
    else:
        low = _dot(v.astype(BF16), v1_ref[...])
        mix = _sigmoid(v0_ref[...] + _dot(low.astype(BF16), v2_ref[...]))
        v = v + (vfirst_ref[...].reshape(n_rows, D_MODEL) - v) * mix
    vbf_s[...] = v.astype(BF16)
    bonus_s[...] = rk_sum * v

    ext = []
    for q in range(n_seqs):
        ext.append(jnp.concatenate([pool_halo_ref[q], u[q * ts:(q + 1) * ts, :]], axis=0))
        pool_halo_ref[q] = u[(q + 1) * ts - POOL_HALO:(q + 1) * ts, :]
    pos = (s_idx * ts + tok + 1).astype(F32)
    mixed = []
    for gi, win in enumerate(POOL_WINDOWS):
        sl = slice(gi * POOL_GROUP, (gi + 1) * POOL_GROUP)
        sums = []
        for q in range(n_seqs):
            acc = ext[q][:, sl]
            step = 1
            while step < win:
                acc = acc + pltpu.roll(acc, step, axis=0)
                step *= 2
            sums.append(acc[POOL_HALO:, :])
        pooled = jnp.concatenate(sums, axis=0) * (1.0 / jnp.minimum(pos, float(win)))
        mixed.append(_dot((pooled - u[:, sl]).astype(BF16), pool_lin_ref[gi]))
    mixed = jnp.concatenate(mixed, axis=1) * pool_scale_ref[...]
    y_pool = _dot((mixed * _silu(z_pool_gate)).astype(BF16), w_pool_proj_ref[...])

    inv_refs = (at_s, rt_s, bt_s, kt_s, vbf_s, tcat_s, arb_s, fvw_s, fvy_s)
    inv_consts = (blk_bf, m_s_cat_ref[...], m_i_cat_ref[...], eye_ref[...])
    st_refs = (at_s, rt_s, vbf_s, tcat_s, arb_s, fvw_s, fvy_s, bend_s, kend_s, plast_s, y_s, state_ref)
    st_consts = (blk_bf, blk_bf.astype(F32))
    n_chunks = ts // CHUNK

    def chunk_rows(q, c):
        return slice(q * ts + c * CHUNK, q * ts + (c + 1) * CHUNK)

    for step in range(n_chunks + 1):
        chains = []
        for q in range(n_seqs):
            if step >= 1:
                chains += [_state_chain(chunk_rows(q, step - 1), q * N_GROUPS + g, g, st_refs, st_consts)
                           for g in range(N_GROUPS)]
        for q in range(n_seqs):
            if step < n_chunks:
                chains += [_inverse_chain(chunk_rows(q, step), ls, inv_refs, inv_consts) for ls in _GROUP_LANES]
        _interleave(chains)

    y = y_s[...]
    inv_n = 1.0 / HEAD_SIZE
    dev = y - _head_sums(y, blk_bf) * inv_n
    var = _head_sums(dev * dev, blk_bf) * inv_n
    y = dev * lax.rsqrt(var + LNX_EPS) * lnx_w_ref[...] + lnx_b_ref[...] + bonus_s[...]
    y_rwkv = _dot((y * _silu(proj(OFF_RWKV_GATE, D_MODEL))).astype(BF16), w_rwkv_ref[...])

    merged = (_sigmoid(proj(OFF_G_POOL, D_MODEL)) * y_pool
              + _sigmoid(proj(OFF_G_RWKV, D_MODEL)) * y_rwkv)
    xo = x + _dot(merged.astype(BF16), w_out_ref[...])
    if last:
        xo = xo * lax.rsqrt(jnp.mean(xo * xo, axis=-1, keepdims=True) + NORM_EPS) * final_g_ref[...]
    out_ref[...] = xo.reshape(n_seqs, ts, D_MODEL)


def _scan_constants():
    idx = np.arange(GROUP_W)
    same_head = (idx[:, None] // HEAD_SIZE) == (idx[None, :] // HEAD_SIZE)
    t_col = idx[None, :] % CHUNK
    t64 = np.arange(CHUNK)[:, None]
    blk = same_head.astype(np.float32)
    m_s_cat = (t_col < t64).astype(np.float32)
    m_i_cat = (t_col <= t64).astype(np.float32)
    eye_cat = (t_col == t64).astype(np.float32)
    return (jnp.asarray(blk, BF16), jnp.asarray(m_s_cat), jnp.asarray(m_i_cat), jnp.asarray(eye_cat))


def _resident(arr):
    nd = arr.ndim
    return pl.BlockSpec(arr.shape, lambda b, s, _nd=nd: (0,) * _nd, pipeline_mode=pl.Buffered(1))


def _layer_call(x, vfirst, weights, consts, *, first, last):
    batch, seq, _ = x.shape
    ts, n_seqs = SEQ_TILE, SEQS_PER_TILE
    assert seq % ts == 0 and batch % n_seqs == 0 and ts % CHUNK == 0 and ts >= POOL_HALO
    n_rows = n_seqs * ts
    tile = pl.BlockSpec((n_seqs, ts, D_MODEL), lambda b, s: (b, s, 0))
    operands = [x] + ([] if first else [vfirst]) + list(weights) + list(consts)
    in_specs = [tile] + ([] if first else [tile]) + [_resident(w) for w in list(weights) + list(consts)]
    act = jax.ShapeDtypeStruct(x.shape, F32)
    out_shape = (act, act) if first else act
    out_specs = (tile, tile) if first else tile
    scratch = [
        pltpu.VMEM((n_seqs * N_GROUPS, GROUP_W, GROUP_W), F32),
        pltpu.VMEM((n_seqs, POOL_HALO, D_MODEL), F32),
        pltpu.VMEM((n_seqs * SUBLANE, SHIFT_WIDTH), F32),
    ]
    scratch += [pltpu.VMEM((n_rows, D_MODEL), BF16) for _ in range(9)]
    scratch += [pltpu.VMEM((n_rows, D_MODEL), F32) for _ in range(5)]
    return pl.pallas_call(
        functools.partial(_layer_kernel, first=first, last=last),
        grid=(batch // n_seqs, seq // ts),
        in_specs=in_specs,
        out_specs=out_specs,
        out_shape=out_shape,
        scratch_shapes=scratch,
        compiler_params=pltpu.CompilerParams(
            dimension_semantics=("arbitrary", "arbitrary"),
            vmem_limit_bytes=VMEM_LIMIT_BYTES),
        name="rwkv_pool_layer",
    )(*operands)


def _pad_rows(w, rows, offset):
    out = jnp.zeros((rows, w.shape[1]), w.dtype)
    return out.at[offset:offset + w.shape[0]].set(w)


def kernel(x, norm_g, w_in, pool_lin, pool_scale, w_pool_proj, mu_shift, decay_w0, decay_w2, a0, a2,
           vres_v0, vres_v1, vres_v2, k_k, k_a, r_k, lnx_w, lnx_b, w_rwkv_proj, w_out, final_g):
    consts = _scan_constants()
    row = lambda p: p.reshape(1, -1).astype(F32)
    vfirst = None
    for i in range(DEPTH):
        first, last = i == 0, i == DEPTH - 1
        weights = [row(norm_g[i]), w_in[i].astype(BF16), pool_lin[i].astype(BF16), row(pool_scale[i]),
                   w_pool_proj[i].astype(BF16), row(mu_shift[i]), row(decay_w0[i]),
                   _pad_rows(decay_w2[i], 2 * D_LORA, 0).astype(BF16), row(a0[i]),
                   _pad_rows(a2[i], 2 * D_LORA, D_LORA).astype(BF16)]
        if not first:
            v1 = jnp.zeros((D_MODEL, LANE), F32).at[:, :D_MV_LORA].set(vres_v1[i - 1])
            weights += [row(vres_v0[i - 1]), v1.astype(BF16),
                        _pad_rows(vres_v2[i - 1], LANE, 0).astype(BF16)]
        weights += [row(k_k[i]), row(k_a[i]), row(r_k[i]), row(lnx_w[i]), row(lnx_b[i]),
                    w_rwkv_proj[i].astype(BF16), w_out[i].astype(BF16)]
        if last:
            weights.append(row(final_g))
        res = _layer_call(x, vfirst, weights, consts, first=first, last=last)
        if first:
            x, vfirst = res
        else:
            x = res
    return x
```

```python
import functools

import numpy as np
import jax
import jax.numpy as jnp
from jax import lax
from jax.experimental import pallas as pl
from jax.experimental.pallas import tpu as pltpu

F32 = jnp.float32
BF16 = jnp.bfloat16

D_MODEL = 1024
DEPTH = 4
POOL_WINDOWS = (2, 4, 8, 16)
POOL_GROUP = D_MODEL // len(POOL_WINDOWS)
POOL_HALO = max(POOL_WINDOWS)
HEAD_SIZE = 64
D_LORA = 64
D_MV_LORA = 32
LANE = 128
SUBLANE = 8
NORM_EPS = 1e-6
LNX_EPS = 1e-5 * HEAD_SIZE
LOG2_DECAY_SCALE = -float(np.exp(-0.5) * np.log2(np.e))

OFF_POOL_IN = 0
OFF_POOL_GATE = D_MODEL
OFF_R = 2 * D_MODEL
OFF_K = 3 * D_MODEL
OFF_V = 4 * D_MODEL
OFF_LORA = 5 * D_MODEL
OFF_RWKV_GATE = OFF_LORA + 2 * D_LORA
OFF_G_POOL = OFF_RWKV_GATE + D_MODEL
OFF_G_RWKV = OFF_G_POOL + D_MODEL
IN_WIDTH = OFF_G_RWKV + D_MODEL
SHIFT_WIDTH = 3 * D_MODEL + 2 * D_LORA

MXU_TILE = 256
GROUP_HEADS = MXU_TILE // HEAD_SIZE
GROUP_W = MXU_TILE
N_GROUPS = D_MODEL // GROUP_W
CHUNK = 64
SEQ_TILE = 128
SEQS_PER_TILE = 2
VMEM_LIMIT_BYTES = 56 * 1024 * 1024


def _sigmoid_of_half(hz):
    return 0.5 * jnp.tanh(hz) + 0.5


def _silu_of_half(hz):
    return hz * jnp.tanh(hz) + hz


def _dot(a, b):
    return jnp.dot(a, b, preferred_element_type=F32)


def _dot_nt(a, b):
    return lax.dot_general(a, b, (((1,), (1,)), ((), ())), preferred_element_type=F32)


def _dot_tn(a, b):
    return lax.dot_general(a, b, (((0,), (0,)), ((), ())), preferred_element_type=F32)


def _head_sums(z, blk_bf):
    outs = [_dot(z[:, ls].astype(BF16), blk_bf) for ls in _GROUP_LANES]
    return jnp.concatenate(outs, axis=1)


def _chunk_cumsum(z, row):
    t_in_chunk = row % CHUNK
    step = 1
    while step < CHUNK:
        z = z + jnp.where(t_in_chunk >= step, pltpu.roll(z, step, axis=0), 0.0)
        step *= 2
    n_chunks = z.shape[0] // CHUNK
    total = jnp.concatenate(
        [jnp.broadcast_to(z[(cc + 1) * CHUNK - 1:(cc + 1) * CHUNK, :], (CHUNK, z.shape[1])) for cc in range(n_chunks)],
        axis=0)
    return z, total


def _tile_rows(z):
    return jnp.concatenate([z] * GROUP_HEADS, axis=0)


def _tile_mask(z, blk_bf):
    return _tile_rows(z) * blk_bf


_GROUP_LANES = [slice(g * GROUP_W, (g + 1) * GROUP_W) for g in range(N_GROUPS)]


def _interleave(chains):
    active = list(chains)
    while active:
        still = []
        for chain in active:
            try:
                next(chain)
                still.append(chain)
            except StopIteration:
                pass
        active = still


def _inverse_chain(rows, ls, refs, consts):
    at_s, rt_s, bt_s, kt_s, vbf_s, tcat_s, arb_s, fvw_s, fvy_s = refs
    blk_bf, m_s_cat, m_i_cat, eye = consts
    lhs = jnp.concatenate([at_s[rows, ls], rt_s[rows, ls]], axis=0)
    rhs = jnp.concatenate([_tile_mask(bt_s[rows, ls], blk_bf), _tile_mask(kt_s[rows, ls], blk_bf)], axis=0)
    a_all = _dot_nt(lhs, rhs)
    yield
    a_ab = a_all[:CHUNK, :GROUP_W] * m_s_cat
    x_acc = eye + a_ab
    p_bf = a_ab.astype(BF16)
    p_bf = _dot(p_bf, _tile_mask(p_bf, blk_bf)).astype(BF16)
    yield
    n_levels = CHUNK.bit_length() - 1
    for level in range(1, n_levels):
        p_bd = _tile_mask(p_bf, blk_bf)
        if level < n_levels - 1:
            both = _dot(jnp.concatenate([x_acc.astype(BF16), p_bf], axis=0), p_bd)
            x_acc = x_acc + both[:CHUNK]
            p_bf = both[CHUNK:].astype(BF16)
        else:
            x_acc = x_acc + _dot(x_acc.astype(BF16), p_bd)
        yield
    tcat_s[rows, ls] = x_acc.astype(BF16)
    arb_s[rows, ls] = (a_all[CHUNK:, :GROUP_W] * m_i_cat).astype(BF16)
    from_v = _dot(jnp.concatenate([a_all[:CHUNK, GROUP_W:] * m_s_cat, a_all[CHUNK:, GROUP_W:] * m_i_cat],
                                  axis=0).astype(BF16), _tile_mask(vbf_s[rows, ls], blk_bf))
    yield
    fvw_s[rows, ls] = from_v[:CHUNK]
    fvy_s[rows, ls] = from_v[CHUNK:]


def _state_chain(rows, state_idx, g, refs, consts):
    at_s, rt_s, vbf_s, tcat_s, arb_s, fvw_s, fvy_s, bend_s, kend_s, plast_s, y_s, state_ref = refs
    blk_bf, blk_f32 = consts
    ls = _GROUP_LANES[g]
    state = state_ref[state_idx]
    lhs = jnp.concatenate([at_s[rows, ls], rt_s[rows, ls]], axis=0)
    from_state = _dot_nt(lhs, state.astype(BF16))
    yield
    yield
    w_bd = _tile_mask((from_state[:CHUNK] + fvw_s[rows, ls]).astype(BF16), blk_bf)
    u_bf = _dot(tcat_s[rows, ls], w_bd).astype(BF16)
    yield
    yield
    y_u = _dot(arb_s[rows, ls], _tile_mask(u_bf, blk_bf))
    upd = _dot_tn(jnp.concatenate([u_bf, vbf_s[rows, ls]], axis=0),
                  jnp.concatenate([bend_s[rows, ls], kend_s[rows, ls]], axis=0))
    yield
    yield
    y_s[rows, ls] = from_state[CHUNK:] + fvy_s[rows, ls] + y_u
    state_ref[state_idx] = state * plast_s[rows.start:rows.start + 1, ls] + upd * blk_f32


def _layer_kernel(*refs, first, last):
    it = iter(refs)
    x_ref = next(it)
    vfirst_ref = None if first else next(it)
    norm_g_ref = next(it)
    w_in_ref = next(it)
    pool_lin_ref = next(it)
    pool_scale_ref = next(it)
    w_pool_proj_ref = next(it)
    mu_ref = next(it)
    w0_ref = next(it)
    w2_ref = next(it)
    a0_ref = next(it)
    a2_ref = next(it)
    if not first:
        v0_ref = next(it)
        v1_ref = next(it)
        v2_ref = next(it)
    kk_scale_ref = next(it)
    ka_ref = next(it)
    rk_ref = next(it)
    lnx_w_ref = next(it)
    lnx_b_ref = next(it)
    w_rwkv_ref = next(it)
    w_out_ref = next(it)
    final_g_ref = next(it) if last else None
    blk_bf_ref = next(it)
    m_s_cat_ref = next(it)
    m_i_cat_ref = next(it)
    eye_ref = next(it)
    out_ref = next(it)
    vfirst_out_ref = next(it) if first else None
    state_ref = next(it)
    pool_halo_ref = next(it)
    shift_halo_ref = next(it)
    at_s, rt_s, bt_s, kt_s, bend_s, kend_s, vbf_s, tcat_s, arb_s, h_s = (next(it) for _ in range(10))
    plast_s, fvw_s, fvy_s, y_s, bonus_s = (next(it) for _ in range(5))

    n_seqs, ts = x_ref.shape[0], x_ref.shape[1]
    n_rows = n_seqs * ts
    s_idx = pl.program_id(1)

    @pl.when(s_idx == 0)
    def _():
        state_ref[...] = jnp.zeros_like(state_ref)
        pool_halo_ref[...] = jnp.zeros_like(pool_halo_ref)
        shift_halo_ref[...] = jnp.zeros_like(shift_halo_ref)

    x = x_ref[...].reshape(n_rows, D_MODEL)
    h_s[...] = (x * lax.rsqrt(jnp.mean(x * x, axis=-1, keepdims=True) + NORM_EPS) * norm_g_ref[...]).astype(BF16)

    def proj(off, width):
        return _dot(h_s[...], w_in_ref[:, off:off + width])

    row = lax.broadcasted_iota(jnp.int32, (n_rows, 1), 0)
    tok = row % ts
    blk_bf = blk_bf_ref[...]

    def token_shift(z, off, width):
        rolled = pltpu.roll(z, 1, axis=0)
        pieces = []
        for q in range(n_seqs):
            halo_row = slice(q * SUBLANE, q * SUBLANE + 1)
            head = jnp.where(row[:SUBLANE] == 0, shift_halo_ref[halo_row, off:off + width],
                             rolled[q * ts:q * ts + SUBLANE, :])
            pieces += [head, rolled[q * ts + SUBLANE:(q + 1) * ts, :]]
            shift_halo_ref[halo_row, off:off + width] = z[(q + 1) * ts - 1:(q + 1) * ts, :]
        prev = jnp.concatenate(pieces, axis=0)
        return z + mu_ref[:, off:off + width] * (prev - z)

    z_lora = proj(OFF_LORA, 2 * D_LORA)
    z_k = proj(OFF_K, D_MODEL)
    lora = token_shift(z_lora, 3 * D_MODEL, 2 * D_LORA)
    lw = LOG2_DECAY_SCALE * _sigmoid_of_half(w0_ref[...] + _dot(jnp.tanh(lora).astype(BF16), w2_ref[...]))
    a = _sigmoid_of_half(a0_ref[...] + _dot(lora.astype(BF16), a2_ref[...]))
    z_r = proj(OFF_R, D_MODEL)
    k = token_shift(z_k, D_MODEL, D_MODEL)
    kk = k * kk_scale_ref[...]
    kk = kk * lax.rsqrt(jnp.maximum(_head_sums(kk * kk, blk_bf), 1e-24))
    k = k * (1.0 + (a - 1.0) * ka_ref[...])
    z_v = proj(OFF_V, D_MODEL)
    cum, total = _chunk_cumsum(lw, row)
    e_neg = jnp.exp2(-cum)
    e_tail = jnp.exp2(total - cum)
    kka = kk * a
    at_s[...] = (-kk * jnp.exp2(cum - lw)).astype(BF16)
    bt_s[...] = (kka * e_neg).astype(BF16)
    kt_s[...] = (k * e_neg).astype(BF16)
    bend_s[...] = (kka * e_tail).astype(BF16)
    kend_s[...] = (k * e_tail).astype(BF16)
    plast_s[...] = jnp.exp2(total)
    u = proj(OFF_POOL_IN, D_MODEL)
    r = token_shift(z_r, 0, D_MODEL)
    rt_s[...] = (r * jnp.exp2(cum)).astype(BF16)
    rk_sum = _head_sums(r * k * rk_ref[...], blk_bf)
    z_pool_gate = proj(OFF_POOL_GATE, D_MODEL)
    v = token_shift(z_v, 2 * D_MODEL, D_MODEL)
    if first:
        vfirst_out_ref[...] = v.reshape(n_seqs, ts, D_MODEL)
    else:
        low = _dot(v.astype(BF16), v1_ref[...])
        mix = _sigmoid_of_half(v0_ref[...] + _dot(low.astype(BF16), v2_ref[...]))
        v = v + (vfirst_ref[...].reshape(n_rows, D_MODEL) - v) * mix
    vbf_s[...] = v.astype(BF16)
    bonus_s[...] = rk_sum * v

    ext = []
    for q in range(n_seqs):
        ext.append(jnp.concatenate([pool_halo_ref[q], u[q * ts:(q + 1) * ts, :]], axis=0))
        pool_halo_ref[q] = u[(q + 1) * ts - POOL_HALO:(q + 1) * ts, :]
    pos = (s_idx * ts + tok + 1).astype(F32)
    mixed = []
    for gi, win in enumerate(POOL_WINDOWS):
        sl = slice(gi * POOL_GROUP, (gi + 1) * POOL_GROUP)
        sums = []
        for q in range(n_seqs):
            acc = ext[q][:, sl]
            step = 1
            while step < win:
                acc = acc + pltpu.roll(acc, step, axis=0)
                step *= 2
            sums.append(acc[POOL_HALO:, :])
        pooled = jnp.concatenate(sums, axis=0) * (1.0 / jnp.minimum(pos, float(win)))
        mixed.append(_dot((pooled - u[:, sl]).astype(BF16), pool_lin_ref[gi]))
    mixed = jnp.concatenate(mixed, axis=1) * pool_scale_ref[...]
    y_pool = _dot((mixed * _silu_of_half(z_pool_gate)).astype(BF16), w_pool_proj_ref[...])

    inv_refs = (at_s, rt_s, bt_s, kt_s, vbf_s, tcat_s, arb_s, fvw_s, fvy_s)
    inv_consts = (blk_bf, m_s_cat_ref[...], m_i_cat_ref[...], eye_ref[...])
    st_refs = (at_s, rt_s, vbf_s, tcat_s, arb_s, fvw_s, fvy_s, bend_s, kend_s, plast_s, y_s, state_ref)
    st_consts = (blk_bf, blk_bf.astype(F32))
    n_chunks = ts // CHUNK

    def chunk_rows(q, c):
        return slice(q * ts + c * CHUNK, q * ts + (c + 1) * CHUNK)

    gates = {}

    def gate_chain(off):
        parts = []
        for j in range(D_MODEL // GROUP_W):
            parts.append(proj(off + j * GROUP_W, GROUP_W))
            yield
        gates[off] = jnp.concatenate(parts, axis=1)

    gate_offsets = (OFF_RWKV_GATE, OFF_G_POOL, OFF_G_RWKV)
    for step in range(n_chunks + 1):
        chains = [gate_chain(off) for i, off in enumerate(gate_offsets) if i % (n_chunks + 1) == step]
        for q in range(n_seqs):
            if step >= 1:
                chains += [_state_chain(chunk_rows(q, step - 1), q * N_GROUPS + g, g, st_refs, st_consts)
                           for g in range(N_GROUPS)]
        for q in range(n_seqs):
            if step < n_chunks:
                chains += [_inverse_chain(chunk_rows(q, step), ls, inv_refs, inv_consts) for ls in _GROUP_LANES]
        _interleave(chains)

    y = y_s[...]
    inv_n = 1.0 / HEAD_SIZE
    dev = y - _head_sums(y, blk_bf) * inv_n
    var = _head_sums(dev * dev, blk_bf) * inv_n
    y = dev * lax.rsqrt(var + LNX_EPS) * lnx_w_ref[...] + lnx_b_ref[...] + bonus_s[...]
    y_rwkv = _dot((y * _silu_of_half(gates[OFF_RWKV_GATE])).astype(BF16), w_rwkv_ref[...])

    merged = _sigmoid_of_half(gates[OFF_G_POOL]) * y_pool + _sigmoid_of_half(gates[OFF_G_RWKV]) * y_rwkv
    xo = x + _dot(merged.astype(BF16), w_out_ref[...])
    if last:
        xo = xo * lax.rsqrt(jnp.mean(xo * xo, axis=-1, keepdims=True) + NORM_EPS) * final_g_ref[...]
    out_ref[...] = xo.reshape(n_seqs, ts, D_MODEL)


def _scan_constants():
    idx = np.arange(GROUP_W)
    same_head = (idx[:, None] // HEAD_SIZE) == (idx[None, :] // HEAD_SIZE)
    t_col = idx[None, :] % CHUNK
    t64 = np.arange(CHUNK)[:, None]
    blk = same_head.astype(np.float32)
    m_s_cat = (t_col < t64).astype(np.float32)
    m_i_cat = (t_col <= t64).astype(np.float32)
    eye_cat = (t_col == t64).astype(np.float32)
    return (jnp.asarray(blk, BF16), jnp.asarray(m_s_cat), jnp.asarray(m_i_cat), jnp.asarray(eye_cat))


def _resident(arr):
    nd = arr.ndim
    return pl.BlockSpec(arr.shape, lambda b, s, _nd=nd: (0,) * _nd, pipeline_mode=pl.Buffered(1))


def _layer_call(x, vfirst, weights, consts, *, first, last):
    batch, seq, _ = x.shape
    ts, n_seqs = SEQ_TILE, SEQS_PER_TILE
    assert seq % ts == 0 and batch % n_seqs == 0 and ts % CHUNK == 0 and ts >= POOL_HALO
    n_rows = n_seqs * ts
    tile = pl.BlockSpec((n_seqs, ts, D_MODEL), lambda b, s: (b, s, 0))
    operands = [x] + ([] if first else [vfirst]) + list(weights) + list(consts)
    in_specs = [tile] + ([] if first else [tile]) + [_resident(w) for w in list(weights) + list(consts)]
    act = jax.ShapeDtypeStruct(x.shape, F32)
    out_shape = (act, act) if first else act
    out_specs = (tile, tile) if first else tile
    scratch = [
        pltpu.VMEM((n_seqs * N_GROUPS, GROUP_W, GROUP_W), F32),
        pltpu.VMEM((n_seqs, POOL_HALO, D_MODEL), F32),
        pltpu.VMEM((n_seqs * SUBLANE, SHIFT_WIDTH), F32),
    ]
    scratch += [pltpu.VMEM((n_rows, D_MODEL), BF16) for _ in range(10)]
    scratch += [pltpu.VMEM((n_rows, D_MODEL), F32) for _ in range(5)]
    return pl.pallas_call(
        functools.partial(_layer_kernel, first=first, last=last),
        grid=(batch // n_seqs, seq // ts),
        in_specs=in_specs,
        out_specs=out_specs,
        out_shape=out_shape,
        scratch_shapes=scratch,
        compiler_params=pltpu.CompilerParams(
            dimension_semantics=("arbitrary", "arbitrary"),
            vmem_limit_bytes=VMEM_LIMIT_BYTES),
        name="rwkv_pool_layer",
    )(*operands)


def _pad_rows(w, rows, offset):
    out = jnp.zeros((rows, w.shape[1]), w.dtype)
    return out.at[offset:offset + w.shape[0]].set(w)


def kernel(x, norm_g, w_in, pool_lin, pool_scale, w_pool_proj, mu_shift, decay_w0, decay_w2, a0, a2,
           vres_v0, vres_v1, vres_v2, k_k, k_a, r_k, lnx_w, lnx_b, w_rwkv_proj, w_out, final_g):
    consts = _scan_constants()
    row = lambda p: p.reshape(1, -1).astype(F32)
    gate_cols = np.zeros((IN_WIDTH,), bool)
    gate_cols[OFF_POOL_GATE:OFF_POOL_GATE + D_MODEL] = True
    gate_cols[OFF_RWKV_GATE:] = True
    in_scale = jnp.asarray(np.where(gate_cols, 0.5, 1.0), F32)
    vfirst = None
    for i in range(DEPTH):
        first, last = i == 0, i == DEPTH - 1
        weights = [row(norm_g[i]), (w_in[i] * in_scale).astype(BF16), pool_lin[i].astype(BF16), row(pool_scale[i]),
                   w_pool_proj[i].astype(BF16), row(mu_shift[i]), row(0.5 * decay_w0[i]),
                   _pad_rows(0.5 * decay_w2[i], 2 * D_LORA, 0).astype(BF16), row(0.5 * a0[i]),
                   _pad_rows(0.5 * a2[i], 2 * D_LORA, D_LORA).astype(BF16)]
        if not first:
            v1 = jnp.zeros((D_MODEL, LANE), F32).at[:, :D_MV_LORA].set(vres_v1[i - 1])
            weights += [row(0.5 * vres_v0[i - 1]), v1.astype(BF16),
                        _pad_rows(0.5 * vres_v2[i - 1], LANE, 0).astype(BF16)]
        weights += [row(k_k[i]), row(k_a[i]), row(r_k[i]), row(lnx_w[i]), row(lnx_b[i]),
                    w_rwkv_proj[i].astype(BF16), w_out[i].astype(BF16)]
        if last:
            weights.append(row(final_g))
        res = _layer_call(x, vfirst, weights, consts, first=first, last=last)
        if first:
            x, vfirst = res
        else:
            x = res
    return x
```

```python
import functools

import numpy as np
import jax
import jax.numpy as jnp
from jax import lax
from jax.experimental import pallas as pl
from jax.experimental.pallas import tpu as pltpu

F32 = jnp.float32
BF16 = jnp.bfloat16

D_MODEL = 1024
DEPTH = 4
POOL_WINDOWS = (2, 4, 8, 16)
POOL_GROUP = D_MODEL // len(POOL_WINDOWS)
POOL_HALO = max(POOL_WINDOWS)
HEAD_SIZE = 64
D_LORA = 64
D_MV_LORA = 32
LANE = 128
SUBLANE = 8
NORM_EPS = 1e-6
LNX_EPS = 1e-5 * HEAD_SIZE
LOG2_DECAY_SCALE = -float(np.exp(-0.5) * np.log2(np.e))

OFF_POOL_IN = 0
OFF_POOL_GATE = D_MODEL
OFF_R = 2 * D_MODEL
OFF_K = 3 * D_MODEL
OFF_V = 4 * D_MODEL
OFF_LORA = 5 * D_MODEL
OFF_RWKV_GATE = OFF_LORA + 2 * D_LORA
OFF_G_POOL = OFF_RWKV_GATE + D_MODEL
OFF_G_RWKV = OFF_G_POOL + D_MODEL
IN_WIDTH = OFF_G_RWKV + D_MODEL
SHIFT_WIDTH = 3 * D_MODEL + 2 * D_LORA

MXU_TILE = 256
GROUP_HEADS = MXU_TILE // HEAD_SIZE
GROUP_W = MXU_TILE
N_GROUPS = D_MODEL // GROUP_W
CHUNK = 64
SEQ_TILE = 128
SEQS_PER_TILE = 4
VMEM_LIMIT_BYTES = 62 * 1024 * 1024


def _sigmoid_of_half(hz):
    return 0.5 * jnp.tanh(hz) + 0.5


def _silu_of_half(hz):
    return hz * jnp.tanh(hz) + hz


def _dot(a, b):
    return jnp.dot(a, b, preferred_element_type=F32)


def _dot_nt(a, b):
    return lax.dot_general(a, b, (((1,), (1,)), ((), ())), preferred_element_type=F32)


def _dot_tn(a, b):
    return lax.dot_general(a, b, (((0,), (0,)), ((), ())), preferred_element_type=F32)


def _head_sums(z, blk_bf):
    outs = [_dot(z[:, ls].astype(BF16), blk_bf) for ls in _GROUP_LANES]
    return jnp.concatenate(outs, axis=1)


def _chunk_cumsum(z, row):
    t_in_chunk = row % CHUNK
    step = 1
    while step < CHUNK:
        z = z + jnp.where(t_in_chunk >= step, pltpu.roll(z, step, axis=0), 0.0)
        step *= 2
    n_chunks = z.shape[0] // CHUNK
    total = jnp.concatenate(
        [jnp.broadcast_to(z[(cc + 1) * CHUNK - 1:(cc + 1) * CHUNK, :], (CHUNK, z.shape[1])) for cc in range(n_chunks)],
        axis=0)
    return z, total


def _tile_rows(z):
    return jnp.concatenate([z] * GROUP_HEADS, axis=0)


def _tile_mask(z, blk_bf):
    return _tile_rows(z) * blk_bf


_GROUP_LANES = [slice(g * GROUP_W, (g + 1) * GROUP_W) for g in range(N_GROUPS)]


def _interleave(chains):
    active = list(chains)
    while active:
        still = []
        for chain in active:
            try:
                next(chain)
                still.append(chain)
            except StopIteration:
                pass
        active = still


def _inverse_chain(rows, ls, refs, consts):
    at_s, rt_s, bt_s, kt_s, vbf_s, tcat_s, arb_s, fvw_s, fvy_s = refs
    blk_bf, m_s_cat, m_i_cat, eye = consts
    lhs = jnp.concatenate([at_s[rows, ls], rt_s[rows, ls]], axis=0)
    rhs = jnp.concatenate([_tile_mask(bt_s[rows, ls], blk_bf), _tile_mask(kt_s[rows, ls], blk_bf)], axis=0)
    a_all = _dot_nt(lhs, rhs)
    yield
    a_ab = a_all[:CHUNK, :GROUP_W] * m_s_cat
    x_acc = eye + a_ab
    p_bf = a_ab.astype(BF16)
    p_bf = _dot(p_bf, _tile_mask(p_bf, blk_bf)).astype(BF16)
    yield
    n_levels = CHUNK.bit_length() - 1
    for level in range(1, n_levels):
        p_bd = _tile_mask(p_bf, blk_bf)
        if level < n_levels - 1:
            both = _dot(jnp.concatenate([x_acc.astype(BF16), p_bf], axis=0), p_bd)
            x_acc = x_acc + both[:CHUNK]
            p_bf = both[CHUNK:].astype(BF16)
        else:
            x_acc = x_acc + _dot(x_acc.astype(BF16), p_bd)
        yield
    tcat_s[rows, ls] = x_acc.astype(BF16)
    arb_s[rows, ls] = (a_all[CHUNK:, :GROUP_W] * m_i_cat).astype(BF16)
    from_v = _dot(jnp.concatenate([a_all[:CHUNK, GROUP_W:] * m_s_cat, a_all[CHUNK:, GROUP_W:] * m_i_cat],
                                  axis=0).astype(BF16), _tile_mask(vbf_s[rows, ls], blk_bf))
    yield
    fvw_s[rows, ls] = from_v[:CHUNK]
    fvy_s[rows, ls] = from_v[CHUNK:]


def _state_chain(rows, state_idx, g, refs, consts):
    at_s, rt_s, vbf_s, tcat_s, arb_s, fvw_s, fvy_s, bend_s, kend_s, plast_s, y_s, state_ref = refs
    blk_bf, blk_f32 = consts
    ls = _GROUP_LANES[g]
    state = state_ref[state_idx]
    lhs = jnp.concatenate([at_s[rows, ls], rt_s[rows, ls]], axis=0)
    from_state = _dot_nt(lhs, state.astype(BF16))
    yield
    yield
    w_bd = _tile_mask((from_state[:CHUNK] + fvw_s[rows, ls]).astype(BF16), blk_bf)
    u_bf = _dot(tcat_s[rows, ls], w_bd).astype(BF16)
    yield
    yield
    y_u = _dot(arb_s[rows, ls], _tile_mask(u_bf, blk_bf))
    upd = _dot_tn(jnp.concatenate([u_bf, vbf_s[rows, ls]], axis=0),
                  jnp.concatenate([bend_s[rows, ls], kend_s[rows, ls]], axis=0))
    yield
    yield
    y_s[rows, ls] = from_state[CHUNK:] + fvy_s[rows, ls] + y_u
    decay_row = rows.start // CHUNK * SUBLANE
    state_ref[state_idx] = state * plast_s[decay_row:decay_row + 1, ls] + upd * blk_f32


def _layer_kernel(*refs, first, last):
    it = iter(refs)
    x_ref = next(it)
    vfirst_ref = None if first else next(it)
    norm_g_ref = next(it)
    w_in_ref = next(it)
    pool_lin_ref = next(it)
    pool_scale_ref = next(it)
    w_pool_proj_ref = next(it)
    mu_ref = next(it)
    w0_ref = next(it)
    w2_ref = next(it)
    a0_ref = next(it)
    a2_ref = next(it)
    if not first:
        v0_ref = next(it)
        v1_ref = next(it)
        v2_ref = next(it)
    kk_scale_ref = next(it)
    ka_ref = next(it)
    rk_ref = next(it)
    lnx_w_ref = next(it)
    lnx_b_ref = next(it)
    w_rwkv_ref = next(it)
    w_out_ref = next(it)
    final_g_ref = next(it) if last else None
    blk_bf_ref = next(it)
    m_s_cat_ref = next(it)
    m_i_cat_ref = next(it)
    eye_ref = next(it)
    out_ref = next(it)
    vfirst_out_ref = next(it) if first else None
    state_ref = next(it)
    pool_halo_ref = next(it)
    shift_halo_ref = next(it)
    at_s, rt_s, bt_s, kt_s, bend_s, kend_s, vbf_s, tcat_s, arb_s, h_s = (next(it) for _ in range(10))
    fvw_s, fvy_s, y_s, bonus_s, plast_s = (next(it) for _ in range(5))

    n_seqs, ts = x_ref.shape[0], x_ref.shape[1]
    n_rows = n_seqs * ts
    s_idx = pl.program_id(1)

    @pl.when(s_idx == 0)
    def _():
        state_ref[...] = jnp.zeros_like(state_ref)
        pool_halo_ref[...] = jnp.zeros_like(pool_halo_ref)
        shift_halo_ref[...] = jnp.zeros_like(shift_halo_ref)

    x = x_ref[...].reshape(n_rows, D_MODEL)
    h_s[...] = (x * lax.rsqrt(jnp.mean(x * x, axis=-1, keepdims=True) + NORM_EPS) * norm_g_ref[...]).astype(BF16)

    def proj(off, width):
        return _dot(h_s[...], w_in_ref[:, off:off + width])

    row = lax.broadcasted_iota(jnp.int32, (n_rows, 1), 0)
    tok = row % ts
    blk_bf = blk_bf_ref[...]

    def token_shift(z, off, width):
        rolled = pltpu.roll(z, 1, axis=0)
        pieces = []
        for q in range(n_seqs):
            halo_row = slice(q * SUBLANE, q * SUBLANE + 1)
            head = jnp.where(row[:SUBLANE] == 0, shift_halo_ref[halo_row, off:off + width],
                             rolled[q * ts:q * ts + SUBLANE, :])
            pieces += [head, rolled[q * ts + SUBLANE:(q + 1) * ts, :]]
            shift_halo_ref[halo_row, off:off + width] = z[(q + 1) * ts - 1:(q + 1) * ts, :]
        prev = jnp.concatenate(pieces, axis=0)
        return z + mu_ref[:, off:off + width] * (prev - z)

    z_lora = proj(OFF_LORA, 2 * D_LORA)
    z_k = proj(OFF_K, D_MODEL)
    lora = token_shift(z_lora, 3 * D_MODEL, 2 * D_LORA)
    lw = LOG2_DECAY_SCALE * _sigmoid_of_half(w0_ref[...] + _dot(jnp.tanh(lora).astype(BF16), w2_ref[...]))
    a = _sigmoid_of_half(a0_ref[...] + _dot(lora.astype(BF16), a2_ref[...]))
    z_r = proj(OFF_R, D_MODEL)
    k = token_shift(z_k, D_MODEL, D_MODEL)
    kk = k * kk_scale_ref[...]
    kk = kk * lax.rsqrt(jnp.maximum(_head_sums(kk * kk, blk_bf), 1e-24))
    k = k * (1.0 + (a - 1.0) * ka_ref[...])
    z_v = proj(OFF_V, D_MODEL)
    cum, total = _chunk_cumsum(lw, row)
    e_neg = jnp.exp2(-cum)
    e_tail = jnp.exp2(total - cum)
    kka = kk * a
    at_s[...] = (-kk * jnp.exp2(cum - lw)).astype(BF16)
    bt_s[...] = (kka * e_neg).astype(BF16)
    kt_s[...] = (k * e_neg).astype(BF16)
    bend_s[...] = (kka * e_tail).astype(BF16)
    kend_s[...] = (k * e_tail).astype(BF16)
    for c in range(n_rows // CHUNK):
        plast_s[c * SUBLANE:c * SUBLANE + 1, :] = jnp.exp2(total[c * CHUNK:c * CHUNK + 1, :])
    r = token_shift(z_r, 0, D_MODEL)
    rt_s[...] = (r * jnp.exp2(cum)).astype(BF16)
    rk_sum = _head_sums(r * k * rk_ref[...], blk_bf)
    v = token_shift(z_v, 2 * D_MODEL, D_MODEL)
    if first:
        vfirst_out_ref[...] = v.reshape(n_seqs, ts, D_MODEL)
    else:
        low = _dot(v.astype(BF16), v1_ref[...])
        mix = _sigmoid_of_half(v0_ref[...] + _dot(low.astype(BF16), v2_ref[...]))
        v = v + (vfirst_ref[...].reshape(n_rows, D_MODEL) - v) * mix
    vbf_s[...] = v.astype(BF16)
    bonus_s[...] = rk_sum * v

    inv_refs = (at_s, rt_s, bt_s, kt_s, vbf_s, tcat_s, arb_s, fvw_s, fvy_s)
    inv_consts = (blk_bf, m_s_cat_ref[...], m_i_cat_ref[...], eye_ref[...])
    st_refs = (at_s, rt_s, vbf_s, tcat_s, arb_s, fvw_s, fvy_s, bend_s, kend_s, plast_s, y_s, state_ref)
    st_consts = (blk_bf, blk_bf.astype(F32))
    n_chunks = ts // CHUNK

    def chunk_rows(q, c):
        return slice(q * ts + c * CHUNK, q * ts + (c + 1) * CHUNK)

    for step in range(n_chunks + 1):
        chains = []
        for q in range(n_seqs):
            if step >= 1:
                chains += [_state_chain(chunk_rows(q, step - 1), q * N_GROUPS + g, g, st_refs, st_consts)
                           for g in range(N_GROUPS)]
        for q in range(n_seqs):
            if step < n_chunks:
                chains += [_inverse_chain(chunk_rows(q, step), ls, inv_refs, inv_consts) for ls in _GROUP_LANES]
        _interleave(chains)

    u = proj(OFF_POOL_IN, D_MODEL)
    ext = []
    for q in range(n_seqs):
        ext.append(jnp.concatenate([pool_halo_ref[q], u[q * ts:(q + 1) * ts, :]], axis=0))
        pool_halo_ref[q] = u[(q + 1) * ts - POOL_HALO:(q + 1) * ts, :]
    pos = (s_idx * ts + tok + 1).astype(F32)
    mixed = []
    for gi, win in enumerate(POOL_WINDOWS):
        sl = slice(gi * POOL_GROUP, (gi + 1) * POOL_GROUP)
        sums = []
        for q in range(n_seqs):
            acc = ext[q][:, sl]
            step = 1
            while step < win:
                acc = acc + pltpu.roll(acc, step, axis=0)
                step *= 2
            sums.append(acc[POOL_HALO:, :])
        pooled = jnp.concatenate(sums, axis=0) * (1.0 / jnp.minimum(pos, float(win)))
        mixed.append(_dot((pooled - u[:, sl]).astype(BF16), pool_lin_ref[gi]))
    mixed = jnp.concatenate(mixed, axis=1) * pool_scale_ref[...]
    y_pool = _dot((mixed * _silu_of_half(proj(OFF_POOL_GATE, D_MODEL))).astype(BF16), w_pool_proj_ref[...])

    y = y_s[...]
    inv_n = 1.0 / HEAD_SIZE
    dev = y - _head_sums(y, blk_bf) * inv_n
    var = _head_sums(dev * dev, blk_bf) * inv_n
    y = dev * lax.rsqrt(var + LNX_EPS) * lnx_w_ref[...] + lnx_b_ref[...] + bonus_s[...]
    y_rwkv = _dot((y * _silu_of_half(proj(OFF_RWKV_GATE, D_MODEL))).astype(BF16), w_rwkv_ref[...])

    merged = (_sigmoid_of_half(proj(OFF_G_POOL, D_MODEL)) * y_pool
              + _sigmoid_of_half(proj(OFF_G_RWKV, D_MODEL)) * y_rwkv)
    xo = x_ref[...].reshape(n_rows, D_MODEL) + _dot(merged.astype(BF16), w_out_ref[...])
    if last:
        xo = xo * lax.rsqrt(jnp.mean(xo * xo, axis=-1, keepdims=True) + NORM_EPS) * final_g_ref[...]
    out_ref[...] = xo.reshape(n_seqs, ts, D_MODEL)


def _scan_constants():
    idx = np.arange(GROUP_W)
    same_head = (idx[:, None] // HEAD_SIZE) == (idx[None, :] // HEAD_SIZE)
    t_col = idx[None, :] % CHUNK
    t64 = np.arange(CHUNK)[:, None]
    blk = same_head.astype(np.float32)
    m_s_cat = (t_col < t64).astype(np.float32)
    m_i_cat = (t_col <= t64).astype(np.float32)
    eye_cat = (t_col == t64).astype(np.float32)
    return (jnp.asarray(blk, BF16), jnp.asarray(m_s_cat), jnp.asarray(m_i_cat), jnp.asarray(eye_cat))


def _resident(arr):
    nd = arr.ndim
    return pl.BlockSpec(arr.shape, lambda b, s, _nd=nd: (0,) * _nd, pipeline_mode=pl.Buffered(1))


def _layer_call(x, vfirst, weights, consts, *, first, last):
    batch, seq, _ = x.shape
    ts, n_seqs = SEQ_TILE, SEQS_PER_TILE
    assert seq % ts == 0 and batch % n_seqs == 0 and ts % CHUNK == 0 and ts >= POOL_HALO
    n_rows = n_seqs * ts
    tile = pl.BlockSpec((n_seqs, ts, D_MODEL), lambda b, s: (b, s, 0))
    operands = [x] + ([] if first else [vfirst]) + list(weights) + list(consts)
    in_specs = [tile] + ([] if first else [tile]) + [_resident(w) for w in list(weights) + list(consts)]
    act = jax.ShapeDtypeStruct(x.shape, F32)
    out_shape = (act, act) if first else act
    out_specs = (tile, tile) if first else tile
    scratch = [
        pltpu.VMEM((n_seqs * N_GROUPS, GROUP_W, GROUP_W), F32),
        pltpu.VMEM((n_seqs, POOL_HALO, D_MODEL), F32),
        pltpu.VMEM((n_seqs * SUBLANE, SHIFT_WIDTH), F32),
    ]
    scratch += [pltpu.VMEM((n_rows, D_MODEL), BF16) for _ in range(10)]
    scratch += [pltpu.VMEM((n_rows, D_MODEL), F32) for _ in range(4)]
    scratch += [pltpu.VMEM((n_rows // CHUNK * SUBLANE, D_MODEL), F32)]
    return pl.pallas_call(
        functools.partial(_layer_kernel, first=first, last=last),
        grid=(batch // n_seqs, seq // ts),
        in_specs=in_specs,
        out_specs=out_specs,
        out_shape=out_shape,
        scratch_shapes=scratch,
        compiler_params=pltpu.CompilerParams(
            dimension_semantics=("arbitrary", "arbitrary"),
            vmem_limit_bytes=VMEM_LIMIT_BYTES),
        name="rwkv_pool_layer",
    )(*operands)


def _pad_rows(w, rows, offset):
    out = jnp.zeros((rows, w.shape[1]), w.dtype)
    return out.at[offset:offset + w.shape[0]].set(w)


def kernel(x, norm_g, w_in, pool_lin, pool_scale, w_pool_proj, mu_shift, decay_w0, decay_w2, a0, a2,
           vres_v0, vres_v1, vres_v2, k_k, k_a, r_k, lnx_w, lnx_b, w_rwkv_proj, w_out, final_g):
    consts = _scan_constants()
    row = lambda p: p.reshape(1, -1).astype(F32)
    gate_cols = np.zeros((IN_WIDTH,), bool)
    gate_cols[OFF_POOL_GATE:OFF_POOL_GATE + D_MODEL] = True
    gate_cols[OFF_RWKV_GATE:] = True
    in_scale = jnp.asarray(np.where(gate_cols, 0.5, 1.0), F32)
    vfirst = None
    for i in range(DEPTH):
        first, last = i == 0, i == DEPTH - 1
        weights = [row(norm_g[i]), (w_in[i] * in_scale).astype(BF16), pool_lin[i].astype(BF16), row(pool_scale[i]),
                   w_pool_proj[i].astype(BF16), row(mu_shift[i]), row(0.5 * decay_w0[i]),
                   _pad_rows(0.5 * decay_w2[i], 2 * D_LORA, 0).astype(BF16), row(0.5 * a0[i]),
                   _pad_rows(0.5 * a2[i], 2 * D_LORA, D_LORA).astype(BF16)]
        if not first:
            v1 = jnp.zeros((D_MODEL, LANE), F32).at[:, :D_MV_LORA].set(vres_v1[i - 1])
            weights += [row(0.5 * vres_v0[i - 1]), v1.astype(BF16),
                        _pad_rows(0.5 * vres_v2[i - 1], LANE, 0).astype(BF16)]
        weights += [row(k_k[i]), row(k_a[i]), row(r_k[i]), row(lnx_w[i]), row(lnx_b[i]),
                    w_rwkv_proj[i].astype(BF16), w_out[i].astype(BF16)]
        if last:
            weights.append(row(final_g))
        res = _layer_call(x, vfirst, weights, consts, first=first, last=last)
        if first:
            x, vfirst = res
        else:
            x = res
    return x
```

```python
import functools

import numpy as np
import jax
import jax.numpy as jnp
from jax import lax
from jax.experimental import pallas as pl
from jax.experimental.pallas import tpu as pltpu

F32 = jnp.float32
BF16 = jnp.bfloat16

D_MODEL = 1024
DEPTH = 4
POOL_WINDOWS = (2, 4, 8, 16)
POOL_GROUP = D_MODEL // len(POOL_WINDOWS)
POOL_HALO = max(POOL_WINDOWS)
HEAD_SIZE = 64
D_LORA = 64
D_MV_LORA = 32
LANE = 128
SUBLANE = 8
NORM_EPS = 1e-6
LNX_EPS = 1e-5 * HEAD_SIZE
LOG2_DECAY_SCALE = -float(np.exp(-0.5) * np.log2(np.e))

OFF_POOL_IN = 0
OFF_POOL_GATE = D_MODEL
OFF_R = 2 * D_MODEL
OFF_K = 3 * D_MODEL
OFF_V = 4 * D_MODEL
OFF_LORA = 5 * D_MODEL
OFF_RWKV_GATE = OFF_LORA + 2 * D_LORA
OFF_G_POOL = OFF_RWKV_GATE + D_MODEL
OFF_G_RWKV = OFF_G_POOL + D_MODEL
IN_WIDTH = OFF_G_RWKV + D_MODEL
SHIFT_WIDTH = 3 * D_MODEL + 2 * D_LORA

MXU_TILE = 256
GROUP_HEADS = MXU_TILE // HEAD_SIZE
GROUP_W = MXU_TILE
N_GROUPS = D_MODEL // GROUP_W
CHUNK = 64
SEQ_TILE = 128
SEQS_PER_TILE = 4
VMEM_LIMIT_BYTES = 62 * 1024 * 1024


def _sigmoid_of_half(hz):
    return 0.5 * jnp.tanh(hz) + 0.5


def _silu_of_half(hz):
    return hz * jnp.tanh(hz) + hz


def _dot(a, b):
    return jnp.dot(a, b, preferred_element_type=F32)


def _dot_nt(a, b):
    return lax.dot_general(a, b, (((1,), (1,)), ((), ())), preferred_element_type=F32)


def _dot_tn(a, b):
    return lax.dot_general(a, b, (((0,), (0,)), ((), ())), preferred_element_type=F32)


def _head_sums(z, blk_bf):
    outs = [_dot(z[:, ls].astype(BF16), blk_bf) for ls in _GROUP_LANES]
    return jnp.concatenate(outs, axis=1)


def _chunk_cumsum(z, row):
    t_in_chunk = row % CHUNK
    step = 1
    while step < CHUNK:
        z = z + jnp.where(t_in_chunk >= step, pltpu.roll(z, step, axis=0), 0.0)
        step *= 2
    n_chunks = z.shape[0] // CHUNK
    total = jnp.concatenate(
        [jnp.broadcast_to(z[(cc + 1) * CHUNK - 1:(cc + 1) * CHUNK, :], (CHUNK, z.shape[1])) for cc in range(n_chunks)],
        axis=0)
    return z, total


def _tile_rows(z):
    return jnp.concatenate([z] * GROUP_HEADS, axis=0)


def _tile_mask(z, blk_bf):
    return _tile_rows(z) * blk_bf


_GROUP_LANES = [slice(g * GROUP_W, (g + 1) * GROUP_W) for g in range(N_GROUPS)]


def _interleave(chains):
    active = list(chains)
    while active:
        still = []
        for chain in active:
            try:
                next(chain)
                still.append(chain)
            except StopIteration:
                pass
        active = still


def _inverse_chain(rows, ls, refs, consts):
    at_s, rt_s, bt_s, kt_s, vbf_s, tcat_s, arb_s, fvw_s, fvy_s = refs
    blk_bf, m_s_cat, m_i_cat, eye = consts
    lhs = jnp.concatenate([at_s[rows, ls], rt_s[rows, ls]], axis=0)
    rhs = jnp.concatenate([_tile_mask(bt_s[rows, ls], blk_bf), _tile_mask(kt_s[rows, ls], blk_bf)], axis=0)
    a_all = _dot_nt(lhs, rhs)
    yield
    a_ab = a_all[:CHUNK, :GROUP_W] * m_s_cat
    x_acc = eye + a_ab
    p_bf = a_ab.astype(BF16)
    p_bf = _dot(p_bf, _tile_mask(p_bf, blk_bf)).astype(BF16)
    yield
    n_levels = CHUNK.bit_length() - 1
    for level in range(1, n_levels):
        p_bd = _tile_mask(p_bf, blk_bf)
        if level < n_levels - 1:
            both = _dot(jnp.concatenate([x_acc.astype(BF16), p_bf], axis=0), p_bd)
            x_acc = x_acc + both[:CHUNK]
            p_bf = both[CHUNK:].astype(BF16)
        else:
            x_acc = x_acc + _dot(x_acc.astype(BF16), p_bd)
        yield
    tcat_s[rows, ls] = x_acc.astype(BF16)
    arb_s[rows, ls] = (a_all[CHUNK:, :GROUP_W] * m_i_cat).astype(BF16)
    from_v = _dot(jnp.concatenate([a_all[:CHUNK, GROUP_W:] * m_s_cat, a_all[CHUNK:, GROUP_W:] * m_i_cat],
                                  axis=0).astype(BF16), _tile_mask(vbf_s[rows, ls], blk_bf))
    yield
    fvw_s[rows, ls] = from_v[:CHUNK]
    fvy_s[rows, ls] = from_v[CHUNK:]


def _state_chain(rows, state_idx, g, refs, consts):
    at_s, rt_s, vbf_s, tcat_s, arb_s, fvw_s, fvy_s, bend_s, kend_s, plast_s, y_s, state_ref = refs
    blk_bf, blk_f32 = consts
    ls = _GROUP_LANES[g]
    state = state_ref[state_idx]
    lhs = jnp.concatenate([at_s[rows, ls], rt_s[rows, ls]], axis=0)
    from_state = _dot_nt(lhs, state.astype(BF16))
    yield
    yield
    w_bd = _tile_mask((from_state[:CHUNK] + fvw_s[rows, ls]).astype(BF16), blk_bf)
    u_bf = _dot(tcat_s[rows, ls], w_bd).astype(BF16)
    yield
    yield
    y_u = _dot(arb_s[rows, ls], _tile_mask(u_bf, blk_bf))
    upd = _dot_tn(jnp.concatenate([u_bf, vbf_s[rows, ls]], axis=0),
                  jnp.concatenate([bend_s[rows, ls], kend_s[rows, ls]], axis=0))
    yield
    yield
    y_s[rows, ls] = from_state[CHUNK:] + fvy_s[rows, ls] + y_u
    decay_row = rows.start // CHUNK * SUBLANE
    state_ref[state_idx] = state * plast_s[decay_row:decay_row + 1, ls] + upd * blk_f32


def _layer_kernel(*refs, first, last):
    it = iter(refs)
    x_ref = next(it)
    vfirst_ref = None if first else next(it)
    norm_g_ref = next(it)
    w_in_ref = next(it)
    pool_lin_ref = next(it)
    pool_scale_ref = next(it)
    w_pool_proj_ref = next(it)
    mu_ref = next(it)
    w0_ref = next(it)
    w2_ref = next(it)
    a0_ref = next(it)
    a2_ref = next(it)
    if not first:
        v0_ref = next(it)
        v1_ref = next(it)
        v2_ref = next(it)
    kk_scale_ref = next(it)
    ka_ref = next(it)
    rk_ref = next(it)
    lnx_w_ref = next(it)
    lnx_b_ref = next(it)
    w_rwkv_ref = next(it)
    w_out_ref = next(it)
    final_g_ref = next(it) if last else None
    blk_bf_ref = next(it)
    m_s_cat_ref = next(it)
    m_i_cat_ref = next(it)
    eye_ref = next(it)
    out_ref = next(it)
    vfirst_out_ref = next(it) if first else None
    state_ref = next(it)
    pool_halo_ref = next(it)
    shift_halo_ref = next(it)
    at_s, rt_s, bt_s, kt_s, bend_s, kend_s, vbf_s, tcat_s, arb_s, h_s = (next(it) for _ in range(10))
    fvw_s, fvy_s, y_s, bonus_s, plast_s = (next(it) for _ in range(5))

    n_seqs, ts = x_ref.shape[0], x_ref.shape[1]
    n_rows = n_seqs * ts
    s_idx = pl.program_id(1)

    @pl.when(s_idx == 0)
    def _():
        state_ref[...] = jnp.zeros_like(state_ref)
        pool_halo_ref[...] = jnp.zeros_like(pool_halo_ref)
        shift_halo_ref[...] = jnp.zeros_like(shift_halo_ref)

    x = x_ref[...].reshape(n_rows, D_MODEL)
    h_s[...] = (x * lax.rsqrt(jnp.mean(x * x, axis=-1, keepdims=True) + NORM_EPS) * norm_g_ref[...]).astype(BF16)

    def proj(off, width):
        return _dot(h_s[...], w_in_ref[:, off:off + width])

    row = lax.broadcasted_iota(jnp.int32, (n_rows, 1), 0)
    tok = row % ts
    blk_bf = blk_bf_ref[...]

    def token_shift(z, off, width):
        rolled = pltpu.roll(z, 1, axis=0)
        pieces = []
        for q in range(n_seqs):
            halo_row = slice(q * SUBLANE, q * SUBLANE + 1)
            head = jnp.where(row[:SUBLANE] == 0, shift_halo_ref[halo_row, off:off + width],
                             rolled[q * ts:q * ts + SUBLANE, :])
            pieces += [head, rolled[q * ts + SUBLANE:(q + 1) * ts, :]]
            shift_halo_ref[halo_row, off:off + width] = z[(q + 1) * ts - 1:(q + 1) * ts, :]
        prev = jnp.concatenate(pieces, axis=0)
        return z + mu_ref[:, off:off + width] * (prev - z)

    z_lora = proj(OFF_LORA, 2 * D_LORA)
    z_k = proj(OFF_K, D_MODEL)
    lora = token_shift(z_lora, 3 * D_MODEL, 2 * D_LORA)
    lw = LOG2_DECAY_SCALE * _sigmoid_of_half(w0_ref[...] + _dot(jnp.tanh(lora).astype(BF16), w2_ref[...]))
    a = _sigmoid_of_half(a0_ref[...] + _dot(lora.astype(BF16), a2_ref[...]))
    z_r = proj(OFF_R, D_MODEL)
    k = token_shift(z_k, D_MODEL, D_MODEL)
    kk = k * kk_scale_ref[...]
    kk = kk * lax.rsqrt(jnp.maximum(_head_sums(kk * kk, blk_bf), 1e-24))
    k = k * (1.0 + (a - 1.0) * ka_ref[...])
    z_v = proj(OFF_V, D_MODEL)
    cum, total = _chunk_cumsum(lw, row)
    e_neg = jnp.exp2(-cum)
    e_tail = jnp.exp2(total - cum)
    kka = kk * a
    at_s[...] = (-kk * jnp.exp2(cum - lw)).astype(BF16)
    bt_s[...] = (kka * e_neg).astype(BF16)
    kt_s[...] = (k * e_neg).astype(BF16)
    bend_s[...] = (kka * e_tail).astype(BF16)
    kend_s[...] = (k * e_tail).astype(BF16)
    for c in range(n_rows // CHUNK):
        plast_s[c * SUBLANE:c * SUBLANE + 1, :] = jnp.exp2(total[c * CHUNK:c * CHUNK + 1, :])
    r = token_shift(z_r, 0, D_MODEL)
    rt_s[...] = (r * jnp.exp2(cum)).astype(BF16)
    rk_sum = _head_sums(r * k * rk_ref[...], blk_bf)
    v = token_shift(z_v, 2 * D_MODEL, D_MODEL)
    if first:
        vfirst_out_ref[...] = v.astype(BF16).reshape(n_seqs, ts, D_MODEL)
    else:
        low = _dot(v.astype(BF16), v1_ref[...])
        mix = _sigmoid_of_half(v0_ref[...] + _dot(low.astype(BF16), v2_ref[...]))
        v = v + (vfirst_ref[...].reshape(n_rows, D_MODEL).astype(F32) - v) * mix
    vbf_s[...] = v.astype(BF16)
    bonus_s[...] = rk_sum * v

    inv_refs = (at_s, rt_s, bt_s, kt_s, vbf_s, tcat_s, arb_s, fvw_s, fvy_s)
    inv_consts = (blk_bf, m_s_cat_ref[...], m_i_cat_ref[...], eye_ref[...])
    st_refs = (at_s, rt_s, vbf_s, tcat_s, arb_s, fvw_s, fvy_s, bend_s, kend_s, plast_s, y_s, state_ref)
    st_consts = (blk_bf, blk_bf.astype(F32))
    n_chunks = ts // CHUNK

    def chunk_rows(q, c):
        return slice(q * ts + c * CHUNK, q * ts + (c + 1) * CHUNK)

    for step in range(n_chunks + 1):
        chains = []
        for q in range(n_seqs):
            if step >= 1:
                chains += [_state_chain(chunk_rows(q, step - 1), q * N_GROUPS + g, g, st_refs, st_consts)
                           for g in range(N_GROUPS)]
        for q in range(n_seqs):
            if step < n_chunks:
                chains += [_inverse_chain(chunk_rows(q, step), ls, inv_refs, inv_consts) for ls in _GROUP_LANES]
        _interleave(chains)

    u = proj(OFF_POOL_IN, D_MODEL)
    ext = []
    for q in range(n_seqs):
        ext.append(jnp.concatenate([pool_halo_ref[q], u[q * ts:(q + 1) * ts, :]], axis=0))
        pool_halo_ref[q] = u[(q + 1) * ts - POOL_HALO:(q + 1) * ts, :]
    pos = (s_idx * ts + tok + 1).astype(F32)
    mixed = []
    for gi, win in enumerate(POOL_WINDOWS):
        sl = slice(gi * POOL_GROUP, (gi + 1) * POOL_GROUP)
        sums = []
        for q in range(n_seqs):
            acc = ext[q][:, sl]
            step = 1
            while step < win:
                acc = acc + pltpu.roll(acc, step, axis=0)
                step *= 2
            sums.append(acc[POOL_HALO:, :])
        pooled = jnp.concatenate(sums, axis=0) * (1.0 / jnp.minimum(pos, float(win)))
        mixed.append(_dot((pooled - u[:, sl]).astype(BF16), pool_lin_ref[gi]))
    mixed = jnp.concatenate(mixed, axis=1) * pool_scale_ref[...]
    y_pool = _dot((mixed * _silu_of_half(proj(OFF_POOL_GATE, D_MODEL))).astype(BF16), w_pool_proj_ref[...])

    y = y_s[...]
    inv_n = 1.0 / HEAD_SIZE
    dev = y - _head_sums(y, blk_bf) * inv_n
    var = _head_sums(dev * dev, blk_bf) * inv_n
    y = dev * lax.rsqrt(var + LNX_EPS) * lnx_w_ref[...] + lnx_b_ref[...] + bonus_s[...]
    y_rwkv = _dot((y * _silu_of_half(proj(OFF_RWKV_GATE, D_MODEL))).astype(BF16), w_rwkv_ref[...])

    merged = (_sigmoid_of_half(proj(OFF_G_POOL, D_MODEL)) * y_pool
              + _sigmoid_of_half(proj(OFF_G_RWKV, D_MODEL)) * y_rwkv)
    xo = x_ref[...].reshape(n_rows, D_MODEL) + _dot(merged.astype(BF16), w_out_ref[...])
    if last:
        xo = xo * lax.rsqrt(jnp.mean(xo * xo, axis=-1, keepdims=True) + NORM_EPS) * final_g_ref[...]
    out_ref[...] = xo.reshape(n_seqs, ts, D_MODEL)


def _scan_constants():
    idx = np.arange(GROUP_W)
    same_head = (idx[:, None] // HEAD_SIZE) == (idx[None, :] // HEAD_SIZE)
    t_col = idx[None, :] % CHUNK
    t64 = np.arange(CHUNK)[:, None]
    blk = same_head.astype(np.float32)
    m_s_cat = (t_col < t64).astype(np.float32)
    m_i_cat = (t_col <= t64).astype(np.float32)
    eye_cat = (t_col == t64).astype(np.float32)
    return (jnp.asarray(blk, BF16), jnp.asarray(m_s_cat), jnp.asarray(m_i_cat), jnp.asarray(eye_cat))


def _resident(arr):
    nd = arr.ndim
    return pl.BlockSpec(arr.shape, lambda b, s, _nd=nd: (0,) * _nd, pipeline_mode=pl.Buffered(1))


def _layer_call(x, vfirst, weights, consts, *, first, last):
    batch, seq, _ = x.shape
    ts, n_seqs = SEQ_TILE, SEQS_PER_TILE
    assert seq % ts == 0 and batch % n_seqs == 0 and ts % CHUNK == 0 and ts >= POOL_HALO
    n_rows = n_seqs * ts
    tile = pl.BlockSpec((n_seqs, ts, D_MODEL), lambda b, s: (b, s, 0))
    operands = [x] + ([] if first else [vfirst]) + list(weights) + list(consts)
    in_specs = [tile] + ([] if first else [tile]) + [_resident(w) for w in list(weights) + list(consts)]
    act = jax.ShapeDtypeStruct(x.shape, F32)
    out_shape = (act, jax.ShapeDtypeStruct(x.shape, BF16)) if first else act
    out_specs = (tile, tile) if first else tile
    scratch = [
        pltpu.VMEM((n_seqs * N_GROUPS, GROUP_W, GROUP_W), F32),
        pltpu.VMEM((n_seqs, POOL_HALO, D_MODEL), F32),
        pltpu.VMEM((n_seqs * SUBLANE, SHIFT_WIDTH), F32),
    ]
    scratch += [pltpu.VMEM((n_rows, D_MODEL), BF16) for _ in range(10)]
    scratch += [pltpu.VMEM((n_rows, D_MODEL), F32) for _ in range(4)]
    scratch += [pltpu.VMEM((n_rows // CHUNK * SUBLANE, D_MODEL), F32)]
    return pl.pallas_call(
        functools.partial(_layer_kernel, first=first, last=last),
        grid=(batch // n_seqs, seq // ts),
        in_specs=in_specs,
        out_specs=out_specs,
        out_shape=out_shape,
        scratch_shapes=scratch,
        compiler_params=pltpu.CompilerParams(
            dimension_semantics=("arbitrary", "arbitrary"),
            vmem_limit_bytes=VMEM_LIMIT_BYTES),
        name="rwkv_pool_layer",
    )(*operands)


def _pad_rows(w, rows, offset):
    out = jnp.zeros((rows, w.shape[1]), w.dtype)
    return out.at[offset:offset + w.shape[0]].set(w)


def kernel(x, norm_g, w_in, pool_lin, pool_scale, w_pool_proj, mu_shift, decay_w0, decay_w2, a0, a2,
           vres_v0, vres_v1, vres_v2, k_k, k_a, r_k, lnx_w, lnx_b, w_rwkv_proj, w_out, final_g):
    consts = _scan_constants()
    row = lambda p: p.reshape(1, -1).astype(F32)
    gate_cols = np.zeros((IN_WIDTH,), bool)
    gate_cols[OFF_POOL_GATE:OFF_POOL_GATE + D_MODEL] = True
    gate_cols[OFF_RWKV_GATE:] = True
    in_scale = jnp.asarray(np.where(gate_cols, 0.5, 1.0), F32)
    vfirst = None
    for i in range(DEPTH):
        first, last = i == 0, i == DEPTH - 1
        weights = [row(norm_g[i]), (w_in[i] * in_scale).astype(BF16), pool_lin[i].astype(BF16), row(pool_scale[i]),
                   w_pool_proj[i].astype(BF16), row(mu_shift[i]), row(0.5 * decay_w0[i]),
                   _pad_rows(0.5 * decay_w2[i], 2 * D_LORA, 0).astype(BF16), row(0.5 * a0[i]),
                   _pad_rows(0.5 * a2[i], 2 * D_LORA, D_LORA).astype(BF16)]
        if not first:
            v1 = jnp.zeros((D_MODEL, LANE), F32).at[:, :D_MV_LORA].set(vres_v1[i - 1])
            weights += [row(0.5 * vres_v0[i - 1]), v1.astype(BF16),
                        _pad_rows(0.5 * vres_v2[i - 1], LANE, 0).astype(BF16)]
        weights += [row(k_k[i]), row(k_a[i]), row(r_k[i]), row(lnx_w[i]), row(lnx_b[i]),
                    w_rwkv_proj[i].astype(BF16), w_out[i].astype(BF16)]
        if last:
            weights.append(row(final_g))
        res = _layer_call(x, vfirst, weights, consts, first=first, last=last)
        if first:
            x, vfirst = res
        else:
            x = res
    return x
```
